```python
import numpy as np
import jax
import jax.numpy as jnp
from jax import lax

D_MODEL = 1024
BATCH = 8
SEQ = 4096
DEPTH = 4

CHUNK = 64
MEM_LEN = 256
EPS = 1e-6
RET_HEADS = 4
RET_QK_DIM = 128
RET_V_DIM = 256
RET_QK = RET_HEADS * RET_QK_DIM
RET_V = RET_HEADS * RET_V_DIM
ROPE_THETA = 10000.0
SSM_INNER = 2 * D_MODEL
SSM_HEAD_DIM = 64
SSM_HEADS = SSM_INNER // SSM_HEAD_DIM
SSM_GROUPS = 8
SSM_HEADS_PER_GROUP = SSM_HEADS // SSM_GROUPS
SSM_STATE = 128
SSM_CONV = 4
SSM_BC = SSM_GROUPS * SSM_STATE
SSM_CONV_DIM = SSM_INNER + 2 * SSM_BC
N_BRANCH = 2
IN_SIZES = (RET_QK, RET_QK, RET_V, RET_V, SSM_INNER, SSM_CONV_DIM, SSM_HEADS, N_BRANCH * D_MODEL)
IN_DIM = RET_QK + RET_QK + RET_V + RET_V + SSM_INNER + SSM_CONV_DIM + SSM_HEADS + N_BRANCH * D_MODEL
XA_HEADS = 4
XA_HEAD_DIM = D_MODEL // XA_HEADS
D_FF = 4 * D_MODEL

kernel_name = 'hybrid_retention_ssd_xattn_trunk'


def _rms(x):
    xf = x.astype(jnp.float32)
    return (xf * lax.rsqrt(jnp.mean(xf * xf, axis=-1, keepdims=True) + EPS)).astype(x.dtype)


def _rmsnorm(x, w):
    return _rms(x) * w


def _rope(t, cos, sin):
    t1, t2 = jnp.split(t, 2, axis=-1)
    return jnp.concatenate([t1 * cos - t2 * sin, t1 * sin + t2 * cos], axis=-1)


def _causal_conv(x, w, bias):
    c = x.shape[-1]
    out = lax.conv_general_dilated(
        x, w[:, None, :].astype(x.dtype), window_strides=(1,), padding=[(SSM_CONV - 1, 0)],
        dimension_numbers=('NWC', 'WIO', 'NWC'), feature_group_count=c)
    return out + bias


def _chunk_scan(ret_q, ret_k, ret_v, ssm_x, ssm_dt, ssm_b, ssm_c, ssm_a):
    b, s = ret_q.shape[:2]
    n_chunks = s // CHUNK

    def to_chunks(t):
        t = t.astype(jnp.float32).reshape((b, n_chunks, CHUNK) + t.shape[2:])
        return jnp.moveaxis(t, 1, 0)

    idx = jnp.arange(CHUNK, dtype=jnp.float32)
    log_gamma = jnp.log1p(-(2.0 ** (-5.0 - jnp.arange(RET_HEADS, dtype=jnp.float32))))
    rel = jnp.abs(idx[:, None] - idx[None, :])
    ret_intra = jnp.exp(log_gamma[:, None, None] * rel)
    ret_q_decay = jnp.exp(log_gamma[None, :] * (idx[:, None] + 1.0))[..., None]
    ret_k_decay = jnp.exp(log_gamma[None, :] * (CHUNK - 1.0 - idx[:, None]))[..., None]
    ret_chunk_decay = jnp.exp(log_gamma * CHUNK)
    a = ssm_a.astype(jnp.float32)

    def step(carry, inp):
        s_ret, h_ssm = carry
        q, k, v, xs, dt, bm, cm = inp
        sc = jnp.einsum('blhd,bmhd->bhlm', q, k) * ret_intra
        y_ret = (jnp.einsum('bhlm,bmhe->blhe', sc, v)
                 + jnp.einsum('blhd,bhde->blhe', q * ret_q_decay, s_ret))
        s_ret = (s_ret * ret_chunk_decay[None, :, None, None]
                 + jnp.einsum('blhd,blhe->bhde', k * ret_k_decay, v))
        cum = jnp.cumsum(dt * a, axis=1)
        cum_h = jnp.moveaxis(cum, 1, -1)
        seg = jnp.exp(-jnp.abs(cum_h[..., :, None] - cum_h[..., None, :]))
        cb = jnp.einsum('blgn,bmgn->bglm', cm, bm)
        xdt = xs * dt[..., None]
        y_ssm = (jnp.einsum('bghlm,bmghp->blghp', cb[:, :, None] * seg, xdt)
                 + jnp.einsum('blgn,bghpn->blghp', cm, h_ssm) * jnp.exp(cum)[..., None])
        cum_last = cum[:, -1]
        h_ssm = (h_ssm * jnp.exp(cum_last)[..., None, None]
                 + jnp.einsum('bmgn,bmghp->bghpn', bm,
                              xdt * jnp.exp(cum_last[:, None] - cum)[..., None]))
        return (s_ret, h_ssm), (y_ret, y_ssm)

    init = (jnp.zeros((b, RET_HEADS, RET_QK_DIM, RET_V_DIM), jnp.float32),
            jnp.zeros((b, SSM_GROUPS, SSM_HEADS_PER_GROUP, SSM_HEAD_DIM, SSM_STATE), jnp.float32))
    xs_in = tuple(to_chunks(t) for t in (ret_q, ret_k, ret_v, ssm_x, ssm_dt, ssm_b, ssm_c))
    _, (y_ret, y_ssm) = lax.scan(step, init, xs_in)

    def from_chunks(t):
        t = jnp.moveaxis(t, 0, 1)
        return t.reshape((b, s) + t.shape[3:])

    return from_chunks(y_ret), from_chunks(y_ssm)


def _hybrid_mixer(u, cos, sin, w_in, b_gate, conv_w, conv_b, dt_bias, a_log, d_skip,
                  ssm_norm, w_br_ret, w_br_ssm, w_out):
    b, s, _ = u.shape
    proj = u @ w_in
    offsets = [int(o) for o in np.cumsum(IN_SIZES)[:-1]]
    q, k, v, g, z, xbc, dt, gates = jnp.split(proj, offsets, axis=-1)
    q = _rope(q.reshape(b, s, RET_HEADS, RET_QK_DIM), cos, sin)
    k = _rope(k.reshape(b, s, RET_HEADS, RET_QK_DIM), cos, sin) * (RET_QK_DIM ** -0.5)
    v = v.reshape(b, s, RET_HEADS, RET_V_DIM)
    xbc = jax.nn.silu(_causal_conv(xbc, conv_w, conv_b))
    xs, bm, cm = jnp.split(xbc, [SSM_INNER, SSM_INNER + SSM_BC], axis=-1)
    xs = xs.reshape(b, s, SSM_GROUPS, SSM_HEADS_PER_GROUP, SSM_HEAD_DIM)
    bm = bm.reshape(b, s, SSM_GROUPS, SSM_STATE)
    cm = cm.reshape(b, s, SSM_GROUPS, SSM_STATE)
    dt = jax.nn.softplus(dt.astype(jnp.float32) + dt_bias.astype(jnp.float32))
    dt = dt.reshape(b, s, SSM_GROUPS, SSM_HEADS_PER_GROUP)
    ssm_a = -jnp.exp(a_log.astype(jnp.float32)).reshape(SSM_GROUPS, SSM_HEADS_PER_GROUP)
    y_ret, y_ssm = _chunk_scan(q, k, v, xs, dt, bm, cm, ssm_a)
    y_ret = _rms(y_ret.astype(u.dtype)).reshape(b, s, RET_V) * jax.nn.silu(g)
    y_ssm = y_ssm.astype(u.dtype) + xs * d_skip.reshape(SSM_GROUPS, SSM_HEADS_PER_GROUP, 1)
    y_ssm = y_ssm.reshape(b, s, SSM_INNER) * jax.nn.silu(z)
    y_ssm = _rms(y_ssm.reshape(b, s, SSM_GROUPS, SSM_INNER // SSM_GROUPS)).reshape(b, s, SSM_INNER) * ssm_norm
    gate_ret, gate_ssm = jnp.split(jax.nn.sigmoid(gates + b_gate), N_BRANCH, axis=-1)
    merged = gate_ret * (y_ret @ w_br_ret) + gate_ssm * (y_ssm @ w_br_ssm)
    return merged @ w_out


def _cross_attention(h, mem_n, wq, wkv, wo):
    b, s, _ = h.shape
    m = mem_n.shape[1]
    q = (h @ wq).reshape(b, s, XA_HEADS, XA_HEAD_DIM)
    k, v = jnp.split(mem_n @ wkv, 2, axis=-1)
    k = k.reshape(b, m, XA_HEADS, XA_HEAD_DIM)
    v = v.reshape(b, m, XA_HEADS, XA_HEAD_DIM)
    scores = jnp.einsum('bshd,bmhd->bhsm', q, k).astype(jnp.float32) * (XA_HEAD_DIM ** -0.5)
    p = jax.nn.softmax(scores, axis=-1).astype(v.dtype)
    o = jnp.einsum('bhsm,bmhd->bshd', p, v).reshape(b, s, D_MODEL)
    return o @ wo


def _sq_relu_mlp(h, w1, w2):
    return jnp.square(jax.nn.relu(h @ w1)) @ w2


def setup_inputs(seed: int = 0) -> dict:
    key = jax.random.key(seed)
    ks = jax.random.split(key, 32)

    def nrm(k, shape, scale):
        return jax.random.normal(k, shape, jnp.float32) * scale

    def gain(k, shape):
        return 1.0 + nrm(k, shape, 0.02)

    L = DEPTH
    x = nrm(ks[0], (BATCH, SEQ, D_MODEL), 1.0)
    mem = nrm(ks[1], (BATCH, MEM_LEN, D_MODEL), 1.0)
    offset = jax.random.randint(ks[2], (BATCH, 1), 0, 8192, dtype=jnp.int32)
    positions = offset + jnp.arange(SEQ, dtype=jnp.int32)[None, :]
    dt0 = jnp.exp(jax.random.uniform(ks[3], (L, SSM_HEADS), jnp.float32)
                  * (jnp.log(0.1) - jnp.log(0.001)) + jnp.log(0.001))
    dt_bias = dt0 + jnp.log(-jnp.expm1(-dt0))
    a_log = jnp.log(jax.random.uniform(ks[4], (L, SSM_HEADS), jnp.float32, 1.0, 16.0))
    return {
        'x': x,
        'mem': mem,
        'positions': positions,
        'norm_mix': gain(ks[5], (L, D_MODEL)),
        'w_in': nrm(ks[6], (L, D_MODEL, IN_DIM), D_MODEL ** -0.5),
        'b_gate': nrm(ks[7], (L, N_BRANCH * D_MODEL), 0.01),
        'conv_w': nrm(ks[8], (L, SSM_CONV, SSM_CONV_DIM), SSM_CONV ** -0.5),
        'conv_b': nrm(ks[9], (L, SSM_CONV_DIM), 0.01),
        'dt_bias': dt_bias,
        'a_log': a_log,
        'd_skip': gain(ks[10], (L, SSM_HEADS)),
        'ssm_norm': gain(ks[11], (L, SSM_INNER)),
        'w_br_ret': nrm(ks[12], (L, RET_V, D_MODEL), RET_V ** -0.5),
        'w_br_ssm': nrm(ks[13], (L, SSM_INNER, D_MODEL), SSM_INNER ** -0.5),
        'w_out': nrm(ks[14], (L, D_MODEL, D_MODEL), D_MODEL ** -0.5),
        'norm_xa': gain(ks[15], (L, D_MODEL)),
        'norm_mem': gain(ks[16], (L, D_MODEL)),
        'xa_wq': nrm(ks[17], (L, D_MODEL, D_MODEL), D_MODEL ** -0.5),
        'xa_wkv': nrm(ks[18], (L, D_MODEL, 2 * D_MODEL), D_MODEL ** -0.5),
        'xa_wo': nrm(ks[19], (L, D_MODEL, D_MODEL), D_MODEL ** -0.5),
        'norm_mlp': gain(ks[20], (L, D_MODEL)),
        'mlp_w1': nrm(ks[21], (L, D_MODEL, D_FF), D_MODEL ** -0.5),
        'mlp_w2': nrm(ks[22], (L, D_FF, D_MODEL), D_FF ** -0.5),
        'norm_final': gain(ks[23], (D_MODEL,)),
    }


def reference(x, mem, positions, norm_mix, w_in, b_gate, conv_w, conv_b, dt_bias, a_log,
              d_skip, ssm_norm, w_br_ret, w_br_ssm, w_out, norm_xa, norm_mem, xa_wq, xa_wkv,
              xa_wo, norm_mlp, mlp_w1, mlp_w2, norm_final):
    inv_freq = ROPE_THETA ** (-jnp.arange(0, RET_QK_DIM, 2, dtype=jnp.float32) / RET_QK_DIM)
    ang = positions.astype(jnp.float32)[..., None] * inv_freq
    cos = jnp.cos(ang)[:, :, None, :].astype(x.dtype)
    sin = jnp.sin(ang)[:, :, None, :].astype(x.dtype)
    for l in range(DEPTH):
        x = x + _hybrid_mixer(_rmsnorm(x, norm_mix[l]), cos, sin, w_in[l], b_gate[l],
                              conv_w[l], conv_b[l], dt_bias[l], a_log[l], d_skip[l],
                              ssm_norm[l], w_br_ret[l], w_br_ssm[l], w_out[l])
        x = x + _cross_attention(_rmsnorm(x, norm_xa[l]), _rmsnorm(mem, norm_mem[l]),
                                 xa_wq[l], xa_wkv[l], xa_wo[l])
        x = x + _sq_relu_mlp(_rmsnorm(x, norm_mlp[l]), mlp_w1[l], mlp_w2[l])
    return _rmsnorm(x, norm_final)
```

```python
import functools

import jax
import jax.numpy as jnp
from jax import lax
from jax.experimental import pallas as pl
from jax.experimental.pallas import tpu as pltpu

F32 = jnp.float32
BF16 = jnp.bfloat16

D_MODEL = 1024
CHUNK = 64
MEM_LEN = 256
EPS = 1e-6
RET_HEADS = 4
RET_QK_DIM = 128
RET_V_DIM = 256
RET_QK = RET_HEADS * RET_QK_DIM
RET_V = RET_HEADS * RET_V_DIM
ROPE_THETA = 10000.0
SSM_INNER = 2 * D_MODEL
SSM_HEAD_DIM = 64
SSM_HEADS = SSM_INNER // SSM_HEAD_DIM
SSM_GROUPS = 8
SSM_HPG = SSM_HEADS // SSM_GROUPS
SSM_GROUP_DIM = SSM_HPG * SSM_HEAD_DIM
SSM_STATE = 128
SSM_CONV = 4
SSM_BC = SSM_GROUPS * SSM_STATE
SSM_CONV_DIM = SSM_INNER + 2 * SSM_BC
XA_HEADS = 4
XA_HEAD_DIM = D_MODEL // XA_HEADS
D_FF = 4 * D_MODEL

LANES = 128
SUBLANES = 8
VMEM_LIMIT = 56 * 1024 * 1024

COL_BLK = 1024
PROJ_DIM = SSM_CONV_DIM + SSM_INNER + 2 * D_MODEL + 2 * RET_QK + RET_V + RET_V
XBC_BLK = 0
Z_BLK = 2
GATE_BLK = 3
QK_BLK = 8
V_BLK = 9
G_BLK = 10

IN_TM = 1024
POST_TM = 512
MLP_TM = 512
FF_BLK = 1024


def _params(*sem):
    return pltpu.CompilerParams(dimension_semantics=sem, vmem_limit_bytes=VMEM_LIMIT)


def _rms(x):
    return x * lax.rsqrt(jnp.mean(x * x, axis=-1, keepdims=True) + EPS)


def _sigmoid(x):
    return 1.0 / (1.0 + jnp.exp(-x))


def _dot(a, b):
    return jnp.dot(a, b, preferred_element_type=F32)


def _dot_nt(a, b):
    return lax.dot_general(a, b, (((1,), (1,)), ((), ())), preferred_element_type=F32)


def _dot_tn(a, b):
    return lax.dot_general(a, b, (((0,), (0,)), ((), ())), preferred_element_type=F32)


def _rope_kernel(pos_ref, freq_ref, cos_ref, sin_ref):
    ang = pos_ref[...] * freq_ref[...]
    cos_ref[...] = jnp.cos(ang)
    sin_ref[...] = jnp.sin(ang)


def _rope_tables(positions):
    b, s = positions.shape
    n = b * s
    half = RET_QK_DIM // 2
    inv_freq = ROPE_THETA ** (-jnp.arange(0, RET_QK_DIM, 2, dtype=F32) / RET_QK_DIM)
    pos = jnp.broadcast_to(positions.astype(F32).reshape(n // 2, 2, 1), (n // 2, 2, half))
    pos = pos.reshape(n // 2, LANES)
    freq = jnp.concatenate([inv_freq, inv_freq]).reshape(1, LANES)
    tm = min(2048, n // 2)
    cos, sin = pl.pallas_call(
        _rope_kernel,
        grid=(n // 2 // tm,),
        in_specs=[pl.BlockSpec((tm, LANES), lambda i: (i, 0)),
                  pl.BlockSpec((1, LANES), lambda i: (0, 0))],
        out_specs=[pl.BlockSpec((tm, LANES), lambda i: (i, 0)),
                   pl.BlockSpec((tm, LANES), lambda i: (i, 0))],
        out_shape=[jax.ShapeDtypeStruct((n // 2, LANES), F32)] * 2,
        compiler_params=_params("parallel"),
        name="rope_tables",
    )(pos, freq)
    cos = cos.reshape(n, half)
    sin = sin.reshape(n, half)
    return jnp.concatenate([cos, cos], axis=-1), jnp.concatenate([-sin, sin], axis=-1)


def _kv_kernel(mem_ref, nw_ref, w_ref, kv_ref):
    h = (_rms(mem_ref[0]) * nw_ref[0]).astype(BF16)
    kv_ref[0, 0] = _dot(h, w_ref[0]).astype(BF16)


def _memory_kv(mem, norm_mem, wkv):
    depth = wkv.shape[0]
    b, m, d = mem.shape
    return pl.pallas_call(
        _kv_kernel,
        grid=(depth, b),
        in_specs=[pl.BlockSpec((1, m, d), lambda l, i: (i, 0, 0)),
                  pl.BlockSpec((1, 1, d), lambda l, i: (l, 0, 0)),
                  pl.BlockSpec((1, d, 2 * d), lambda l, i: (l, 0, 0))],
        out_specs=pl.BlockSpec((1, 1, m, 2 * d), lambda l, i: (l, i, 0, 0)),
        out_shape=jax.ShapeDtypeStruct((depth, b, m, 2 * d), BF16),
        compiler_params=_params("arbitrary", "arbitrary"),
        name="memory_kv",
    )(mem, norm_mem.reshape(depth, 1, d), wkv)


def _in_proj_kernel(x_ref, nw_ref, w_ref, wdt_ref, o_ref, dt_ref, h_scr):
    @pl.when(pl.program_id(1) == 0)
    def _():
        h = (_rms(x_ref[...]) * nw_ref[...]).astype(BF16)
        h_scr[...] = h
        dt_ref[...] = _dot(h, wdt_ref[...])

    o_ref[...] = _dot(h_scr[...], w_ref[...]).astype(BF16)


def _in_proj(x, nw, w_main, w_dt):
    n, d = x.shape
    return pl.pallas_call(
        _in_proj_kernel,
        grid=(n // IN_TM, PROJ_DIM // COL_BLK),
        in_specs=[pl.BlockSpec((IN_TM, d), lambda i, j: (i, 0)),
                  pl.BlockSpec((1, d), lambda i, j: (0, 0)),
                  pl.BlockSpec((d, COL_BLK), lambda i, j: (0, j)),
                  pl.BlockSpec((d, LANES), lambda i, j: (0, 0))],
        out_specs=[pl.BlockSpec((IN_TM, COL_BLK), lambda i, j: (i, j)),
                   pl.BlockSpec((IN_TM, LANES), lambda i, j: (i, 0))],
        out_shape=[jax.ShapeDtypeStruct((n, PROJ_DIM), BF16),
                   jax.ShapeDtypeStruct((n, LANES), F32)],
        scratch_shapes=[pltpu.VMEM((IN_TM, d), BF16)],
        compiler_params=_params("arbitrary", "arbitrary"),
        name="in_proj",
    )(x, nw, w_main, w_dt)


def _scan_kernel(xbc_ref, z_ref, qk_ref, v_ref, g_ref, dt_ref, cos_ref, sin_ref,
                 convw_ref, convb_ref, dtb_ref, alog_ref, dskip_ref, snorm_ref,
                 intra_ref, qdec_ref, kdec_ref, cdec_ref, ltri_ref,
                 yret_ref, yssm_ref, s_scr, h_scr, conv_scr, act_scr):
    L = CHUNK

    @pl.when(pl.program_id(1) == 0)
    def _():
        s_scr[...] = jnp.zeros_like(s_scr)
        h_scr[...] = jnp.zeros_like(h_scr)
        conv_scr[0:SUBLANES, :] = jnp.zeros((SUBLANES, SSM_CONV_DIM), F32)

    cos = cos_ref[...]
    sin = sin_ref[...]
    for h in range(RET_HEADS):
        qs = slice(h * RET_QK_DIM, (h + 1) * RET_QK_DIM)
        ks = slice(RET_QK + h * RET_QK_DIM, RET_QK + (h + 1) * RET_QK_DIM)
        vs = slice(h * RET_V_DIM, (h + 1) * RET_V_DIM)
        q = qk_ref[:, qs].astype(F32)
        k = qk_ref[:, ks].astype(F32)
        q = q * cos + pltpu.roll(q, RET_QK_DIM // 2, 1) * sin
        k = (k * cos + pltpu.roll(k, RET_QK_DIM // 2, 1) * sin) * (RET_QK_DIM ** -0.5)
        v = v_ref[:, vs]
        sc = _dot_nt(q.astype(BF16), k.astype(BF16)) * intra_ref[h]
        s_prev = s_scr[h]
        y = (_dot(sc.astype(BF16), v)
             + _dot((q * qdec_ref[:, qs]).astype(BF16), s_prev.astype(BF16)))
        s_scr[h] = (s_prev * cdec_ref[h:h + 1, :]
                    + _dot_tn((k * kdec_ref[:, qs]).astype(BF16), v))
        g = g_ref[:, vs].astype(F32)
        yret_ref[:, vs] = (_rms(y) * (g * _sigmoid(g))).astype(BF16)

    conv_scr[SUBLANES:SUBLANES + L, :] = xbc_ref[...].astype(F32)
    for cb in range(SSM_CONV_DIM // 512):
        cs = slice(cb * 512, (cb + 1) * 512)
        acc = convb_ref[:, cs] + convw_ref[0:1, cs] * conv_scr[SUBLANES - 3:SUBLANES - 3 + L, cs]
        for j in range(1, SSM_CONV):
            off = SUBLANES - (SSM_CONV - 1) + j
            acc = acc + convw_ref[j:j + 1, cs] * conv_scr[off:off + L, cs]
        act_scr[:, cs] = acc * _sigmoid(acc)
    conv_scr[0:SUBLANES, :] = conv_scr[L:L + SUBLANES, :]

    dtp = dt_ref[...] + dtb_ref[...]
    dt = jnp.maximum(dtp, 0.0) + jnp.log1p(jnp.exp(-jnp.abs(dtp)))
    a = -jnp.exp(alog_ref[...])
    da = dt * a
    cum = jnp.dot(ltri_ref[...], da, preferred_element_type=F32,
                  precision=lax.Precision.HIGHEST)
    cum_t = jnp.transpose(jnp.concatenate([cum, jnp.zeros_like(cum)], axis=0))
    cum_last = cum[L - 1:L, :]
    e_last = jnp.exp(cum_last)
    for gi in range(SSM_GROUPS):
        bs = slice(SSM_INNER + gi * SSM_STATE, SSM_INNER + (gi + 1) * SSM_STATE)
        cs = slice(SSM_INNER + SSM_BC + gi * SSM_STATE, SSM_INNER + SSM_BC + (gi + 1) * SSM_STATE)
        bm = act_scr[:, bs].astype(BF16)
        cm = act_scr[:, cs].astype(BF16)
        cbm = _dot_nt(cm, bm)
        h_prev = h_scr[gi]
        y_state = _dot(cm, h_prev.astype(BF16))
        for hj in range(SSM_HPG):
            hh = gi * SSM_HPG + hj
            xsl = slice(gi * SSM_GROUP_DIM + hj * SSM_HEAD_DIM,
                        gi * SSM_GROUP_DIM + (hj + 1) * SSM_HEAD_DIM)
            psl = slice(hj * SSM_HEAD_DIM, (hj + 1) * SSM_HEAD_DIM)
            xs = act_scr[:, xsl]
            col = cum[:, hh:hh + 1]
            row = cum_t[hh:hh + 1, 0:L]
            seg = jnp.exp(-jnp.abs(col - row))
            xdt = xs * dt[:, hh:hh + 1]
            y = (_dot((cbm * seg).astype(BF16), xdt.astype(BF16))
                 + y_state[:, psl] * jnp.exp(col))
            xw = xdt * jnp.exp(cum_last[:, hh:hh + 1] - col)
            h_scr[gi, :, psl] = (h_prev[:, psl] * e_last[:, hh:hh + 1]
                                 + _dot_tn(bm, xw.astype(BF16)))
            act_scr[:, xsl] = y + xs * dskip_ref[:, xsl]
        gs = slice(gi * SSM_GROUP_DIM, (gi + 1) * SSM_GROUP_DIM)
        zz = z_ref[:, gs].astype(F32)
        yg = act_scr[:, gs] * (zz * _sigmoid(zz))
        yssm_ref[:, gs] = (_rms(yg) * snorm_ref[:, gs]).astype(BF16)


def _chunk_scan(proj, dt_raw, cos2, sin2, conv_w, conv_b, dt_bias, a_log, d_skip, ssm_norm,
                tables, batch):
    n = proj.shape[0]
    nc = n // batch // CHUNK
    intra, qdec, kdec, cdec, ltri = tables
    L = CHUNK

    def row(width, blk):
        return pl.BlockSpec((L, width), lambda b, c: (b * nc + c, blk))

    def full(shape):
        return pl.BlockSpec(shape, lambda b, c: (0,) * len(shape))

    return pl.pallas_call(
        _scan_kernel,
        grid=(batch, nc),
        in_specs=[row(SSM_CONV_DIM, XBC_BLK), row(SSM_INNER, Z_BLK), row(2 * RET_QK, QK_BLK),
                  row(RET_V, V_BLK), row(RET_V, G_BLK), row(LANES, 0), row(LANES, 0), row(LANES, 0),
                  full((SSM_CONV, SSM_CONV_DIM)), full((1, SSM_CONV_DIM)), full((1, LANES)),
                  full((1, LANES)), full((1, SSM_INNER)), full((1, SSM_INNER)),
                  full((RET_HEADS, L, L)), full((L, RET_QK)), full((L, RET_QK)),
                  full((RET_HEADS, RET_V_DIM)), full((L, L))],
        out_specs=[row(RET_V, 0), row(SSM_INNER, 0)],
        out_shape=[jax.ShapeDtypeStruct((n, RET_V), BF16),
                   jax.ShapeDtypeStruct((n, SSM_INNER), BF16)],
        scratch_shapes=[pltpu.VMEM((RET_HEADS, RET_QK_DIM, RET_V_DIM), F32),
                        pltpu.VMEM((SSM_GROUPS, SSM_STATE, SSM_GROUP_DIM), F32),
                        pltpu.VMEM((L + SUBLANES, SSM_CONV_DIM), F32),
                        pltpu.VMEM((L, SSM_CONV_DIM), F32)],
        compiler_params=_params("arbitrary", "arbitrary"),
        name="chunk_scan",
    )(proj, proj, proj, proj, proj, dt_raw, cos2, sin2, conv_w, conv_b, dt_bias, a_log,
      d_skip, ssm_norm, intra, qdec, kdec, cdec, ltri)


def _scan_tables():
    idx = jnp.arange(CHUNK, dtype=F32)
    log_gamma = jnp.log1p(-(2.0 ** (-5.0 - jnp.arange(RET_HEADS, dtype=F32))))
    rel = jnp.abs(idx[:, None] - idx[None, :])
    intra = jnp.exp(log_gamma[:, None, None] * rel)
    qdec = jnp.exp(log_gamma[None, :] * (idx[:, None] + 1.0))
    kdec = jnp.exp(log_gamma[None, :] * (CHUNK - 1.0 - idx[:, None]))
    cdec = jnp.exp(log_gamma * CHUNK)
    qdec = jnp.repeat(qdec, RET_QK_DIM, axis=1)
    kdec = jnp.repeat(kdec, RET_QK_DIM, axis=1)
    cdec = jnp.broadcast_to(cdec[:, None], (RET_HEADS, RET_V_DIM))
    ltri = (idx[:, None] >= idx[None, :]).astype(F32)
    return intra, qdec, kdec, cdec, ltri


def _post_kernel(yret_ref, yssm_ref, gate_ref, x_ref, bg_ref, wr_ref, ws_ref, wo_ref,
                 nxa_ref, wq_ref, kv_ref, xo_ref, out_ref):
    sg = _sigmoid(gate_ref[...].astype(F32) + bg_ref[...])
    merged = (sg[:, :D_MODEL] * _dot(yret_ref[...], wr_ref[...])
              + sg[:, D_MODEL:] * _dot(yssm_ref[...], ws_ref[...]))
    x1 = x_ref[...] + _dot(merged.astype(BF16), wo_ref[...])
    q = _dot((_rms(x1) * nxa_ref[...]).astype(BF16), wq_ref[...]).astype(BF16)
    acc = x1
    for h in range(XA_HEADS):
        hs = slice(h * XA_HEAD_DIM, (h + 1) * XA_HEAD_DIM)
        vs = slice(D_MODEL + h * XA_HEAD_DIM, D_MODEL + (h + 1) * XA_HEAD_DIM)
        s = _dot_nt(q[:, hs], kv_ref[0, 0, :, hs]) * (XA_HEAD_DIM ** -0.5)
        p = jnp.exp(s - jnp.max(s, axis=-1, keepdims=True))
        o = _dot(p.astype(BF16), kv_ref[0, 0, :, vs]) / jnp.sum(p, axis=-1, keepdims=True)
        acc = acc + _dot(o.astype(BF16), xo_ref[hs, :])
    out_ref[...] = acc


def _post(yret, yssm, proj, x, b_gate, wr, ws, wo, nxa, wq, kv, layer, xo, batch):
    n, d = x.shape
    nt = n // batch // POST_TM
    tm = POST_TM

    def row(width, blk=0):
        return pl.BlockSpec((tm, width), lambda b, t: (b * nt + t, blk))

    def full(shape):
        return pl.BlockSpec(shape, lambda b, t: (0,) * len(shape))

    return pl.pallas_call(
        _post_kernel,
        grid=(batch, nt),
        in_specs=[row(RET_V), row(SSM_INNER), row(2 * d, GATE_BLK), row(d),
                  full((1, 2 * d)), full((RET_V, d)), full((SSM_INNER, d)), full((d, d)),
                  full((1, d)), full((d, d)),
                  pl.BlockSpec((1, 1, MEM_LEN, 2 * d), lambda b, t: (layer, b, 0, 0)),
                  full((d, d))],
        out_specs=row(d),
        out_shape=jax.ShapeDtypeStruct((n, d), F32),
        compiler_params=_params("arbitrary", "arbitrary"),
        name="merge_xattn",
    )(yret, yssm, proj, x, b_gate, wr, ws, wo, nxa, wq, kv, xo)


def _mlp_kernel(x_ref, nw_ref, w1_ref, w2_ref, nf_ref, out_ref, *, final_norm):
    x = x_ref[...]
    h = (_rms(x) * nw_ref[...]).astype(BF16)
    acc = x
    for c in range(D_FF // FF_BLK):
        fs = slice(c * FF_BLK, (c + 1) * FF_BLK)
        a = jnp.maximum(_dot(h, w1_ref[:, fs]), 0.0)
        acc = acc + _dot((a * a).astype(BF16), w2_ref[fs, :])
    if final_norm:
        acc = _rms(acc) * nf_ref[...]
    out_ref[...] = acc


def _mlp(x, nw, w1, w2, nf, final_norm):
    n, d = x.shape
    tm = MLP_TM
    return pl.pallas_call(
        functools.partial(_mlp_kernel, final_norm=final_norm),
        grid=(n // tm,),
        in_specs=[pl.BlockSpec((tm, d), lambda i: (i, 0)),
                  pl.BlockSpec((1, d), lambda i: (0, 0)),
                  pl.BlockSpec((d, D_FF), lambda i: (0, 0)),
                  pl.BlockSpec((D_FF, d), lambda i: (0, 0)),
                  pl.BlockSpec((1, d), lambda i: (0, 0))],
        out_specs=pl.BlockSpec((tm, d), lambda i: (i, 0)),
        out_shape=jax.ShapeDtypeStruct((n, d), F32),
        compiler_params=_params("arbitrary"),
        name="sq_relu_mlp",
    )(x, nw, w1, w2, nf)


def kernel(x, mem, positions, norm_mix, w_in, b_gate, conv_w, conv_b, dt_bias, a_log, d_skip,
           ssm_norm, w_br_ret, w_br_ssm, w_out, norm_xa, norm_mem, xa_wq, xa_wkv, xa_wo,
           norm_mlp, mlp_w1, mlp_w2, norm_final):
    b, s, d = x.shape
    depth = w_in.shape[0]
    n = b * s
    assert d == D_MODEL and s % POST_TM == 0 and n % IN_TM == 0 and mem.shape[1] == MEM_LEN

    o_v = 2 * RET_QK
    o_g = o_v + RET_V
    o_z = o_g + RET_V
    o_xbc = o_z + SSM_INNER
    o_dt = o_xbc + SSM_CONV_DIM
    o_gate = o_dt + SSM_HEADS

    cos2, sin2 = _rope_tables(positions)
    kv = _memory_kv(mem, norm_mem, xa_wkv.astype(BF16))
    tables = _scan_tables()
    pad_h = LANES - SSM_HEADS

    xf = x.reshape(n, d)
    for l in range(depth):
        wl = w_in[l]
        w_main = jnp.concatenate(
            [wl[:, o_xbc:o_dt], wl[:, o_z:o_xbc], wl[:, o_gate:], wl[:, :o_z]], axis=1).astype(BF16)
        w_dt = jnp.pad(wl[:, o_dt:o_gate], ((0, 0), (0, pad_h))).astype(BF16)
        proj, dt_raw = _in_proj(xf, norm_mix[l].reshape(1, d), w_main, w_dt)
        yret, yssm = _chunk_scan(
            proj, dt_raw, cos2, sin2, conv_w[l], conv_b[l].reshape(1, -1),
            jnp.pad(dt_bias[l], (0, pad_h)).reshape(1, LANES),
            jnp.pad(a_log[l], (0, pad_h)).reshape(1, LANES),
            jnp.repeat(d_skip[l], SSM_HEAD_DIM).reshape(1, SSM_INNER),
            ssm_norm[l].reshape(1, SSM_INNER), tables, b)
        xf = _post(yret, yssm, proj, xf, b_gate[l].reshape(1, -1), w_br_ret[l].astype(BF16),
                   w_br_ssm[l].astype(BF16), w_out[l].astype(BF16), norm_xa[l].reshape(1, d),
                   xa_wq[l].astype(BF16), kv, l, xa_wo[l].astype(BF16), b)
        xf = _mlp(xf, norm_mlp[l].reshape(1, d), mlp_w1[l].astype(BF16), mlp_w2[l].astype(BF16),
                  norm_final.reshape(1, d), final_norm=(l == depth - 1))
    return xf.reshape(b, s, d)
```

```python
import functools

import jax
import jax.numpy as jnp
from jax import lax
from jax.experimental import pallas as pl
from jax.experimental.pallas import tpu as pltpu

F32 = jnp.float32
BF16 = jnp.bfloat16

D_MODEL = 1024
CHUNK = 64
MEM_LEN = 256
EPS = 1e-6
RET_HEADS = 4
RET_QK_DIM = 128
RET_V_DIM = 256
RET_QK = RET_HEADS * RET_QK_DIM
RET_V = RET_HEADS * RET_V_DIM
ROPE_THETA = 10000.0
SSM_INNER = 2 * D_MODEL
SSM_HEAD_DIM = 64
SSM_HEADS = SSM_INNER // SSM_HEAD_DIM
SSM_GROUPS = 8
SSM_HPG = SSM_HEADS // SSM_GROUPS
SSM_GROUP_DIM = SSM_HPG * SSM_HEAD_DIM
SSM_STATE = 128
SSM_CONV = 4
SSM_BC = SSM_GROUPS * SSM_STATE
SSM_CONV_DIM = SSM_INNER + 2 * SSM_BC
XA_HEADS = 4
XA_HEAD_DIM = D_MODEL // XA_HEADS
D_FF = 4 * D_MODEL

LANES = 128
SUBLANES = 8
VMEM_LIMIT = 56 * 1024 * 1024

COL_BLK = 1024
PROJ_DIM = SSM_CONV_DIM + SSM_INNER + 2 * D_MODEL + 2 * RET_QK + RET_V + RET_V
N_XBC_BLK = SSM_CONV_DIM // COL_BLK
Z_BLK0 = N_XBC_BLK
GATE_BLK0 = Z_BLK0 + SSM_INNER // COL_BLK
QK_BLK = GATE_BLK0 + 2 * D_MODEL // COL_BLK
V_BLK = QK_BLK + 1
G_BLK = V_BLK + 1
HEAD_COPIES = 3

IN_TM = 1024
SCAN_T = 256
POST_TM = 512
MLP_TM = 512
FF_BLK = 1024


def _params(*sem):
    return pltpu.CompilerParams(dimension_semantics=sem, vmem_limit_bytes=VMEM_LIMIT)


def _rms(x):
    return x * lax.rsqrt(jnp.mean(x * x, axis=-1, keepdims=True) + EPS)


def _sigmoid(x):
    return 1.0 / (1.0 + jnp.exp(-x))


def _dot(a, b):
    return jnp.dot(a, b, preferred_element_type=F32)


def _dot_nt(a, b):
    return lax.dot_general(a, b, (((1,), (1,)), ((), ())), preferred_element_type=F32)


def _dot_tn(a, b):
    return lax.dot_general(a, b, (((0,), (0,)), ((), ())), preferred_element_type=F32)


def _rope_kernel(pos_ref, freq_ref, cos_ref, sin_ref):
    ang = pos_ref[...] * freq_ref[...]
    cos_ref[...] = jnp.cos(ang)
    sin_ref[...] = jnp.sin(ang)


def _rope_tables(positions):
    b, s = positions.shape
    n = b * s
    half = RET_QK_DIM // 2
    inv_freq = ROPE_THETA ** (-jnp.arange(0, RET_QK_DIM, 2, dtype=F32) / RET_QK_DIM)
    pos = jnp.broadcast_to(positions.astype(F32).reshape(n // 2, 2, 1), (n // 2, 2, half))
    pos = pos.reshape(n // 2, LANES)
    freq = jnp.concatenate([inv_freq, inv_freq]).reshape(1, LANES)
    tm = min(2048, n // 2)
    cos, sin = pl.pallas_call(
        _rope_kernel,
        grid=(n // 2 // tm,),
        in_specs=[pl.BlockSpec((tm, LANES), lambda i: (i, 0)),
                  pl.BlockSpec((1, LANES), lambda i: (0, 0))],
        out_specs=[pl.BlockSpec((tm, LANES), lambda i: (i, 0)),
                   pl.BlockSpec((tm, LANES), lambda i: (i, 0))],
        out_shape=[jax.ShapeDtypeStruct((n // 2, LANES), F32)] * 2,
        compiler_params=_params("parallel"),
        name="rope_tables",
    )(pos, freq)
    cos = cos.reshape(n, half)
    sin = sin.reshape(n, half)
    return jnp.concatenate([cos, cos], axis=-1), jnp.concatenate([-sin, sin], axis=-1)


def _kv_kernel(mem_ref, nw_ref, w_ref, kv_ref):
    h = (_rms(mem_ref[0]) * nw_ref[0]).astype(BF16)
    kv_ref[0, 0] = _dot(h, w_ref[0]).astype(BF16)


def _memory_kv(mem, norm_mem, wkv):
    depth = wkv.shape[0]
    b, m, d = mem.shape
    return pl.pallas_call(
        _kv_kernel,
        grid=(depth, b),
        in_specs=[pl.BlockSpec((1, m, d), lambda l, i: (i, 0, 0)),
                  pl.BlockSpec((1, 1, d), lambda l, i: (l, 0, 0)),
                  pl.BlockSpec((1, d, 2 * d), lambda l, i: (l, 0, 0))],
        out_specs=pl.BlockSpec((1, 1, m, 2 * d), lambda l, i: (l, i, 0, 0)),
        out_shape=jax.ShapeDtypeStruct((depth, b, m, 2 * d), BF16),
        compiler_params=_params("arbitrary", "arbitrary"),
        name="memory_kv",
    )(mem, norm_mem.reshape(depth, 1, d), wkv)


def _in_proj_kernel(x_ref, nw_ref, w_ref, wdt_ref, cos_ref, sin_ref, convw_ref, convb_ref,
                    bg_ref, o_ref, dt_ref, h_scr, tail_scr, *, tiles_per_seq):
    i = pl.program_id(0)
    j = pl.program_id(1)

    @pl.when(j == 0)
    def _():
        h = (_rms(x_ref[...]) * nw_ref[...]).astype(BF16)
        h_scr[...] = h
        dt_ref[...] = _dot(h, wdt_ref[...])

    def matmul():
        return _dot(h_scr[...], w_ref[...])

    @pl.when(j < N_XBC_BLK)
    def _():
        jt = jnp.minimum(j, N_XBC_BLK - 1)

        @pl.when(i % tiles_per_seq == 0)
        def _():
            tail_scr[jt] = jnp.zeros((SUBLANES, COL_BLK), F32)

        acc = matmul()
        ext = jnp.concatenate([tail_scr[jt], acc], axis=0)
        t = convw_ref[0:1, :] * ext
        for tap in range(1, SSM_CONV):
            t = convw_ref[tap:tap + 1, :] * ext + pltpu.roll(t, 1, 0)
        y = t[SUBLANES:, :] + convb_ref[...]
        o_ref[...] = (y * _sigmoid(y)).astype(BF16)
        tail_scr[jt] = acc[IN_TM - SUBLANES:, :]

    @pl.when(jnp.logical_or(jnp.logical_and(j >= Z_BLK0, j < GATE_BLK0), j == G_BLK))
    def _():
        acc = matmul()
        o_ref[...] = (acc * _sigmoid(acc)).astype(BF16)

    @pl.when(jnp.logical_and(j >= GATE_BLK0, j < QK_BLK))
    def _():
        o_ref[...] = _sigmoid(matmul() + bg_ref[...]).astype(BF16)

    @pl.when(j == QK_BLK)
    def _():
        acc = matmul()
        cos = cos_ref[...]
        sin = sin_ref[...]
        for h in range(2 * RET_HEADS):
            hs = slice(h * RET_QK_DIM, (h + 1) * RET_QK_DIM)
            t = acc[:, hs]
            r = t * cos + pltpu.roll(t, RET_QK_DIM // 2, 1) * sin
            if h >= RET_HEADS:
                r = r * (RET_QK_DIM ** -0.5)
            o_ref[:, hs] = r.astype(BF16)

    @pl.when(j == V_BLK)
    def _():
        o_ref[...] = matmul().astype(BF16)


def _in_proj(x, nw, w_main, w_dt, cos2, sin2, conv_w, conv_b, b_gate, seq):
    n, d = x.shape
    nblk = PROJ_DIM // COL_BLK
    return pl.pallas_call(
        functools.partial(_in_proj_kernel, tiles_per_seq=seq // IN_TM),
        grid=(n // IN_TM, nblk),
        in_specs=[pl.BlockSpec((IN_TM, d), lambda i, j: (i, 0)),
                  pl.BlockSpec((1, d), lambda i, j: (0, 0)),
                  pl.BlockSpec((d, COL_BLK), lambda i, j: (0, j)),
                  pl.BlockSpec((d, LANES), lambda i, j: (0, 0)),
                  pl.BlockSpec((IN_TM, LANES), lambda i, j: (i, 0)),
                  pl.BlockSpec((IN_TM, LANES), lambda i, j: (i, 0)),
                  pl.BlockSpec((SSM_CONV, COL_BLK), lambda i, j: (0, jnp.minimum(j, N_XBC_BLK - 1))),
                  pl.BlockSpec((1, COL_BLK), lambda i, j: (0, jnp.minimum(j, N_XBC_BLK - 1))),
                  pl.BlockSpec((1, COL_BLK), lambda i, j: (0, jnp.clip(j - GATE_BLK0, 0, 1)))],
        out_specs=[pl.BlockSpec((IN_TM, COL_BLK), lambda i, j: (i, j)),
                   pl.BlockSpec((IN_TM, LANES), lambda i, j: (i, 0))],
        out_shape=[jax.ShapeDtypeStruct((n, PROJ_DIM), BF16),
                   jax.ShapeDtypeStruct((n, LANES), F32)],
        scratch_shapes=[pltpu.VMEM((IN_TM, d), BF16),
                        pltpu.VMEM((N_XBC_BLK, SUBLANES, COL_BLK), F32)],
        compiler_params=_params("arbitrary", "arbitrary"),
        name="in_proj",
    )(x, nw, w_main, w_dt, cos2, sin2, conv_w, conv_b, b_gate)


def _split3(x, lane):
    hi = x.astype(BF16).astype(F32)
    r1 = x - hi
    mid = r1.astype(BF16).astype(F32)
    lo = r1 - mid
    return jnp.where(lane < SSM_HEADS, hi, jnp.where(lane < 2 * SSM_HEADS, mid, lo)).astype(BF16)


def _scan_kernel(xbc_ref, zs_ref, qk_ref, v_ref, gs_ref, dt_ref,
                 dtb_ref, alog_ref, dskip_ref, snorm_ref,
                 intra_ref, qdec_ref, kdec_ref, cdec_ref, tri_ref, expand_ref, irep_ref, hmask_ref,
                 yret_ref, yssm_ref, s_scr, h_scr, col_scr, xdt_scr):
    L = CHUNK
    T = SCAN_T
    GD = SSM_GROUP_DIM

    @pl.when(pl.program_id(1) == 0)
    def _():
        s_scr[...] = jnp.zeros_like(s_scr)
        h_scr[...] = jnp.zeros_like(h_scr)

    dtp = dt_ref[...] + dtb_ref[...]
    dt = jnp.maximum(dtp, 0.0) + jnp.log1p(jnp.exp(-jnp.abs(dtp)))
    da = dt * (-jnp.exp(alog_ref[...]))
    cum = jnp.dot(tri_ref[...], da, preferred_element_type=F32,
                  precision=lax.Precision.HIGHEST)
    lane = lax.broadcasted_iota(jnp.int32, (T, LANES), 1)
    parts = jnp.concatenate([_split3(cum, lane), _split3(dt, lane)], axis=0)
    for gi in range(SSM_GROUPS):
        gs = slice(gi * GD, (gi + 1) * GD)
        col = _dot(parts, expand_ref[:, gs])
        col_scr[:, gs] = col[:T]
        xdt_scr[:, gs] = (xbc_ref[:, gs].astype(F32) * col[T:]).astype(BF16)

    def chunk(c, carry):
        rows = pl.ds(pl.multiple_of(c * L, L), L)

        for h in range(RET_HEADS):
            qs = slice(h * RET_QK_DIM, (h + 1) * RET_QK_DIM)
            ks = slice(RET_QK + h * RET_QK_DIM, RET_QK + (h + 1) * RET_QK_DIM)
            vs = slice(h * RET_V_DIM, (h + 1) * RET_V_DIM)
            q = qk_ref[rows, qs]
            k = qk_ref[rows, ks]
            v = v_ref[rows, vs]
            sc = _dot_nt(q, k) * intra_ref[h]
            s_prev = s_scr[h]
            y = _dot(sc.astype(BF16), v) + _dot(q, s_prev.astype(BF16)) * qdec_ref[h]
            kd = (k.astype(F32) * kdec_ref[:, qs]).astype(BF16)
            s_scr[h] = s_prev * cdec_ref[h:h + 1, :] + _dot_tn(kd, v)
            yret_ref[rows, vs] = (_rms(y) * gs_ref[rows, vs].astype(F32)).astype(BF16)

        for gi in range(SSM_GROUPS):
            gs = slice(gi * GD, (gi + 1) * GD)
            bs = slice(SSM_INNER + gi * SSM_STATE, SSM_INNER + (gi + 1) * SSM_STATE)
            cs = slice(SSM_INNER + SSM_BC + gi * SSM_STATE, SSM_INNER + SSM_BC + (gi + 1) * SSM_STATE)
            colc = col_scr[rows, gs]
            rowc = jnp.sum(colc * irep_ref[...], axis=0, keepdims=True)
            seg = jnp.exp(-jnp.abs(colc - rowc))
            last = colc[L - 1:L, :]
            bm = xbc_ref[rows, bs]
            cm = xbc_ref[rows, cs]
            cb = _dot_nt(cm, jnp.concatenate([bm] * SSM_HPG, axis=0))
            xdt = xdt_scr[rows, gs]
            xblk = jnp.concatenate([xdt * hmask_ref[hj:hj + 1, :] for hj in range(SSM_HPG)],
                                   axis=0)
            h_prev = h_scr[gi]
            y = (_dot((cb * seg).astype(BF16), xblk)
                 + _dot(cm, h_prev.astype(BF16)) * jnp.exp(colc))
            yg = ((y + xbc_ref[rows, gs].astype(F32) * dskip_ref[:, gs])
                  * zs_ref[rows, gs].astype(F32))
            yssm_ref[rows, gs] = (_rms(yg) * snorm_ref[:, gs]).astype(BF16)
            xw = (xdt.astype(F32) * jnp.exp(last - colc)).astype(BF16)
            h_scr[gi] = h_prev * jnp.exp(last) + _dot_tn(bm, xw)
        return carry

    lax.fori_loop(0, T // L, chunk, 0)


def _chunk_scan(proj, dt_raw, dt_bias, a_log, d_skip, ssm_norm, tables, batch):
    n = proj.shape[0]
    T = SCAN_T
    nt = n // batch // T
    L = CHUNK

    def row(width, blk):
        return pl.BlockSpec((T, width), lambda b, c: (b * nt + c, blk))

    def full(shape):
        return pl.BlockSpec(shape, lambda b, c: (0,) * len(shape))

    tspecs = [full(t.shape) for t in tables]
    return pl.pallas_call(
        _scan_kernel,
        grid=(batch, nt),
        in_specs=[row(SSM_CONV_DIM, 0), row(SSM_INNER, Z_BLK0 * COL_BLK // SSM_INNER),
                  row(2 * RET_QK, QK_BLK), row(RET_V, V_BLK), row(RET_V, G_BLK), row(LANES, 0),
                  full((1, LANES)), full((1, LANES)), full((1, SSM_INNER)), full((1, SSM_INNER))]
        + tspecs,
        out_specs=[row(RET_V, 0), row(SSM_INNER, 0)],
        out_shape=[jax.ShapeDtypeStruct((n, RET_V), BF16),
                   jax.ShapeDtypeStruct((n, SSM_INNER), BF16)],
        scratch_shapes=[pltpu.VMEM((RET_HEADS, RET_QK_DIM, RET_V_DIM), F32),
                        pltpu.VMEM((SSM_GROUPS, SSM_STATE, SSM_GROUP_DIM), F32),
                        pltpu.VMEM((T, SSM_INNER), F32),
                        pltpu.VMEM((T, SSM_INNER), BF16)],
        compiler_params=_params("arbitrary", "arbitrary"),
        name="chunk_scan",
    )(proj, proj, proj, proj, proj, dt_raw, dt_bias, a_log, d_skip, ssm_norm, *tables)


def _scan_tables():
    L = CHUNK
    idx = jnp.arange(L, dtype=F32)
    log_gamma = jnp.log1p(-(2.0 ** (-5.0 - jnp.arange(RET_HEADS, dtype=F32))))
    rel = jnp.abs(idx[:, None] - idx[None, :])
    intra = jnp.exp(log_gamma[:, None, None] * rel)
    qdec = jnp.exp(log_gamma[:, None] * (idx[None, :] + 1.0))
    qdec = jnp.broadcast_to(qdec[:, :, None], (RET_HEADS, L, RET_V_DIM))
    kdec = jnp.exp(log_gamma[None, :] * (L - 1.0 - idx[:, None]))
    kdec = jnp.repeat(kdec, RET_QK_DIM, axis=1)
    cdec = jnp.broadcast_to(jnp.exp(log_gamma * L)[:, None], (RET_HEADS, RET_V_DIM))
    t = jnp.arange(SCAN_T)
    tri = ((t[:, None] // L == t[None, :] // L) & (t[None, :] <= t[:, None])).astype(F32)
    r = jnp.arange(LANES)
    c = jnp.arange(SSM_INNER)
    expand = ((r[:, None] % SSM_HEADS == c[None, :] // SSM_HEAD_DIM)
              & (r[:, None] < HEAD_COPIES * SSM_HEADS)).astype(BF16)
    gl = jnp.arange(SSM_GROUP_DIM)
    irep = (gl[None, :] % SSM_HEAD_DIM == jnp.arange(L)[:, None]).astype(F32)
    hmask = (gl[None, :] // SSM_HEAD_DIM == jnp.arange(SSM_HPG)[:, None]).astype(BF16)
    return intra, qdec, kdec, cdec, tri, expand, irep, hmask


def _post_kernel(yret_ref, yssm_ref, gate_ref, x_ref, wr_ref, ws_ref, wo_ref,
                 nxa_ref, wq_ref, kv_ref, xo_ref, out_ref):
    merged = (gate_ref[:, :D_MODEL].astype(F32) * _dot(yret_ref[...], wr_ref[...])
              + gate_ref[:, D_MODEL:].astype(F32) * _dot(yssm_ref[...], ws_ref[...]))
    x1 = x_ref[...] + _dot(merged.astype(BF16), wo_ref[...])
    q = _dot((_rms(x1) * nxa_ref[...]).astype(BF16), wq_ref[...]).astype(BF16)
    acc = x1
    for h in range(XA_HEADS):
        hs = slice(h * XA_HEAD_DIM, (h + 1) * XA_HEAD_DIM)
        vs = slice(D_MODEL + h * XA_HEAD_DIM, D_MODEL + (h + 1) * XA_HEAD_DIM)
        s = _dot_nt(q[:, hs], kv_ref[0, 0, :, hs]) * (XA_HEAD_DIM ** -0.5)
        p = jnp.exp(s - jnp.max(s, axis=-1, keepdims=True))
        o = _dot(p.astype(BF16), kv_ref[0, 0, :, vs]) / jnp.sum(p, axis=-1, keepdims=True)
        acc = acc + _dot(o.astype(BF16), xo_ref[hs, :])
    out_ref[...] = acc


def _post(yret, yssm, proj, x, wr, ws, wo, nxa, wq, kv, layer, xo, batch):
    n, d = x.shape
    nt = n // batch // POST_TM
    tm = POST_TM

    def row(width, blk=0):
        return pl.BlockSpec((tm, width), lambda b, t: (b * nt + t, blk))

    def full(shape):
        return pl.BlockSpec(shape, lambda b, t: (0,) * len(shape))

    return pl.pallas_call(
        _post_kernel,
        grid=(batch, nt),
        in_specs=[row(RET_V), row(SSM_INNER), row(2 * d, GATE_BLK0 * COL_BLK // (2 * d)), row(d),
                  full((RET_V, d)), full((SSM_INNER, d)), full((d, d)),
                  full((1, d)), full((d, d)),
                  pl.BlockSpec((1, 1, MEM_LEN, 2 * d), lambda b, t: (layer, b, 0, 0)),
                  full((d, d))],
        out_specs=row(d),
        out_shape=jax.ShapeDtypeStruct((n, d), F32),
        compiler_params=_params("arbitrary", "arbitrary"),
        name="merge_xattn",
    )(yret, yssm, proj, x, wr, ws, wo, nxa, wq, kv, xo)


def _mlp_kernel(x_ref, nw_ref, w1_ref, w2_ref, nf_ref, out_ref, *, final_norm):
    x = x_ref[...]
    h = (_rms(x) * nw_ref[...]).astype(BF16)
    acc = x
    for c in range(D_FF // FF_BLK):
        fs = slice(c * FF_BLK, (c + 1) * FF_BLK)
        a = jnp.maximum(_dot(h, w1_ref[:, fs]), 0.0)
        acc = acc + _dot((a * a).astype(BF16), w2_ref[fs, :])
    if final_norm:
        acc = _rms(acc) * nf_ref[...]
    out_ref[...] = acc


def _mlp(x, nw, w1, w2, nf, final_norm):
    n, d = x.shape
    tm = MLP_TM
    return pl.pallas_call(
        functools.partial(_mlp_kernel, final_norm=final_norm),
        grid=(n // tm,),
        in_specs=[pl.BlockSpec((tm, d), lambda i: (i, 0)),
                  pl.BlockSpec((1, d), lambda i: (0, 0)),
                  pl.BlockSpec((d, D_FF), lambda i: (0, 0)),
                  pl.BlockSpec((D_FF, d), lambda i: (0, 0)),
                  pl.BlockSpec((1, d), lambda i: (0, 0))],
        out_specs=pl.BlockSpec((tm, d), lambda i: (i, 0)),
        out_shape=jax.ShapeDtypeStruct((n, d), F32),
        compiler_params=_params("arbitrary"),
        name="sq_relu_mlp",
    )(x, nw, w1, w2, nf)


def _replicate_heads(v):
    return jnp.pad(jnp.tile(v, HEAD_COPIES), (0, LANES - HEAD_COPIES * SSM_HEADS)).reshape(1, LANES)


def kernel(x, mem, positions, norm_mix, w_in, b_gate, conv_w, conv_b, dt_bias, a_log, d_skip,
           ssm_norm, w_br_ret, w_br_ssm, w_out, norm_xa, norm_mem, xa_wq, xa_wkv, xa_wo,
           norm_mlp, mlp_w1, mlp_w2, norm_final):
    b, s, d = x.shape
    depth = w_in.shape[0]
    n = b * s
    assert d == D_MODEL and s % IN_TM == 0 and mem.shape[1] == MEM_LEN

    o_z = 2 * RET_QK + 2 * RET_V
    o_xbc = o_z + SSM_INNER
    o_dt = o_xbc + SSM_CONV_DIM
    o_gate = o_dt + SSM_HEADS

    cos2, sin2 = _rope_tables(positions)
    kv = _memory_kv(mem, norm_mem, xa_wkv.astype(BF16))
    tables = _scan_tables()

    xf = x.reshape(n, d)
    for l in range(depth):
        wl = w_in[l]
        w_main = jnp.concatenate(
            [wl[:, o_xbc:o_dt], wl[:, o_z:o_xbc], wl[:, o_gate:], wl[:, :o_z]], axis=1).astype(BF16)
        w_dt = jnp.pad(jnp.tile(wl[:, o_dt:o_gate], (1, HEAD_COPIES)),
                       ((0, 0), (0, LANES - HEAD_COPIES * SSM_HEADS))).astype(BF16)
        proj, dt_raw = _in_proj(xf, norm_mix[l].reshape(1, d), w_main, w_dt, cos2, sin2,
                                conv_w[l], conv_b[l].reshape(1, -1), b_gate[l].reshape(1, -1), s)
        yret, yssm = _chunk_scan(
            proj, dt_raw, _replicate_heads(dt_bias[l]), _replicate_heads(a_log[l]),
            jnp.repeat(d_skip[l], SSM_HEAD_DIM).reshape(1, SSM_INNER),
            ssm_norm[l].reshape(1, SSM_INNER), tables, b)
        xf = _post(yret, yssm, proj, xf, w_br_ret[l].astype(BF16), w_br_ssm[l].astype(BF16),
                   w_out[l].astype(BF16), norm_xa[l].reshape(1, d), xa_wq[l].astype(BF16), kv, l,
                   xa_wo[l].astype(BF16), b)
        xf = _mlp(xf, norm_mlp[l].reshape(1, d), mlp_w1[l].astype(BF16), mlp_w2[l].astype(BF16),
                  norm_final.reshape(1, d), final_norm=(l == depth - 1))
    return xf.reshape(b, s, d)
```

```python
import functools

import jax
import jax.numpy as jnp
from jax import lax
from jax.experimental import pallas as pl
from jax.experimental.pallas import tpu as pltpu

F32 = jnp.float32
BF16 = jnp.bfloat16

D_MODEL = 1024
CHUNK = 64
MEM_LEN = 256
EPS = 1e-6
RET_HEADS = 4
RET_QK_DIM = 128
RET_V_DIM = 256
RET_QK = RET_HEADS * RET_QK_DIM
RET_V = RET_HEADS * RET_V_DIM
ROPE_THETA = 10000.0
SSM_INNER = 2 * D_MODEL
SSM_HEAD_DIM = 64
SSM_HEADS = SSM_INNER // SSM_HEAD_DIM
SSM_GROUPS = 8
SSM_HPG = SSM_HEADS // SSM_GROUPS
SSM_GROUP_DIM = SSM_HPG * SSM_HEAD_DIM
SSM_STATE = 128
SSM_CONV = 4
assert SSM_CONV == 4
LOG2_E = 1.4426950408889634
SSM_BC = SSM_GROUPS * SSM_STATE
SSM_CONV_DIM = SSM_INNER + 2 * SSM_BC
XA_HEADS = 4
XA_HEAD_DIM = D_MODEL // XA_HEADS
D_FF = 4 * D_MODEL

LANES = 128
SUBLANES = 8
VMEM_LIMIT = 56 * 1024 * 1024

COL_BLK = 1024
PROJ_DIM = SSM_CONV_DIM + SSM_INNER + 2 * D_MODEL + 2 * RET_QK + RET_V + RET_V
N_XBC_BLK = SSM_CONV_DIM // COL_BLK
Z_BLK0 = N_XBC_BLK
GATE_BLK0 = Z_BLK0 + SSM_INNER // COL_BLK
QK_BLK = GATE_BLK0 + 2 * D_MODEL // COL_BLK
V_BLK = QK_BLK + 1
G_BLK = V_BLK + 1
HEAD_COPIES = 3

IN_TM = 1024
SCAN_T = 256
POST_TM = 512
MLP_TM = 512
FF_BLK = 1024


def _params(*sem):
    return pltpu.CompilerParams(dimension_semantics=sem, vmem_limit_bytes=VMEM_LIMIT)


def _rms(x):
    return x * lax.rsqrt(jnp.mean(x * x, axis=-1, keepdims=True) + EPS)


def _sigmoid(x):
    return 0.5 * jnp.tanh(0.5 * x) + 0.5


def _dot(a, b):
    return jnp.dot(a, b, preferred_element_type=F32)


def _dot_nt(a, b):
    return lax.dot_general(a, b, (((1,), (1,)), ((), ())), preferred_element_type=F32)


def _dot_tn(a, b):
    return lax.dot_general(a, b, (((0,), (0,)), ((), ())), preferred_element_type=F32)


def _rope_kernel(pos_ref, freq_ref, cos_ref, sin_ref):
    ang = pos_ref[...] * freq_ref[...]
    cos_ref[...] = jnp.cos(ang)
    sin_ref[...] = jnp.sin(ang)


def _rope_tables(positions):
    b, s = positions.shape
    n = b * s
    half = RET_QK_DIM // 2
    inv_freq = ROPE_THETA ** (-jnp.arange(0, RET_QK_DIM, 2, dtype=F32) / RET_QK_DIM)
    pos = jnp.broadcast_to(positions.astype(F32).reshape(n // 2, 2, 1), (n // 2, 2, half))
    pos = pos.reshape(n // 2, LANES)
    freq = jnp.concatenate([inv_freq, inv_freq]).reshape(1, LANES)
    tm = min(2048, n // 2)
    cos, sin = pl.pallas_call(
        _rope_kernel,
        grid=(n // 2 // tm,),
        in_specs=[pl.BlockSpec((tm, LANES), lambda i: (i, 0)),
                  pl.BlockSpec((1, LANES), lambda i: (0, 0))],
        out_specs=[pl.BlockSpec((tm, LANES), lambda i: (i, 0)),
                   pl.BlockSpec((tm, LANES), lambda i: (i, 0))],
        out_shape=[jax.ShapeDtypeStruct((n // 2, LANES), F32)] * 2,
        compiler_params=_params("parallel"),
        name="rope_tables",
    )(pos, freq)
    cos = cos.reshape(n, half)
    sin = sin.reshape(n, half)
    return jnp.concatenate([cos, cos], axis=-1), jnp.concatenate([-sin, sin], axis=-1)


def _kv_kernel(mem_ref, nw_ref, w_ref, kv_ref):
    h = (_rms(mem_ref[0]) * nw_ref[0]).astype(BF16)
    kv_ref[0, 0] = _dot(h, w_ref[0]).astype(BF16)


def _memory_kv(mem, norm_mem, wkv):
    depth = wkv.shape[0]
    b, m, d = mem.shape
    return pl.pallas_call(
        _kv_kernel,
        grid=(depth, b),
        in_specs=[pl.BlockSpec((1, m, d), lambda l, i: (i, 0, 0)),
                  pl.BlockSpec((1, 1, d), lambda l, i: (l, 0, 0)),
                  pl.BlockSpec((1, d, 2 * d), lambda l, i: (l, 0, 0))],
        out_specs=pl.BlockSpec((1, 1, m, 2 * d), lambda l, i: (l, i, 0, 0)),
        out_shape=jax.ShapeDtypeStruct((depth, b, m, 2 * d), BF16),
        compiler_params=_params("arbitrary", "arbitrary"),
        name="memory_kv",
    )(mem, norm_mem.reshape(depth, 1, d), wkv)


def _in_proj_kernel(x_ref, nw_ref, w_ref, wdt_ref, cos_ref, sin_ref, convw_ref, convb_ref,
                    bg_ref, o_ref, dt_ref, h_scr, tail_scr, *, tiles_per_seq):
    i = pl.program_id(0)
    j = pl.program_id(1)

    @pl.when(j == 0)
    def _():
        h = (_rms(x_ref[...]) * nw_ref[...]).astype(BF16)
        h_scr[...] = h
        dt_ref[...] = _dot(h, wdt_ref[...])

    def matmul():
        return _dot(h_scr[...], w_ref[...])

    @pl.when(j < N_XBC_BLK)
    def _():
        jt = jnp.minimum(j, N_XBC_BLK - 1)

        @pl.when(i % tiles_per_seq == 0)
        def _():
            tail_scr[jt] = jnp.zeros((SUBLANES, COL_BLK), F32)

        acc = matmul()
        ext = jnp.concatenate([tail_scr[jt], acc], axis=0)
        ext1 = pltpu.roll(ext, 1, 0)
        older = convw_ref[1:2, :] * ext + convw_ref[0:1, :] * ext1
        t = convw_ref[3:4, :] * ext + convw_ref[2:3, :] * ext1 + pltpu.roll(older, 2, 0)
        y = t[SUBLANES:, :] + convb_ref[...]
        o_ref[...] = (y * _sigmoid(y)).astype(BF16)
        tail_scr[jt] = acc[IN_TM - SUBLANES:, :]

    @pl.when(jnp.logical_or(jnp.logical_and(j >= Z_BLK0, j < GATE_BLK0), j == G_BLK))
    def _():
        acc = matmul()
        o_ref[...] = (acc * _sigmoid(acc)).astype(BF16)

    @pl.when(jnp.logical_and(j >= GATE_BLK0, j < QK_BLK))
    def _():
        o_ref[...] = _sigmoid(matmul() + bg_ref[...]).astype(BF16)

    @pl.when(j == QK_BLK)
    def _():
        acc = matmul()
        cos = cos_ref[...]
        sin = sin_ref[...]
        for h in range(2 * RET_HEADS):
            hs = slice(h * RET_QK_DIM, (h + 1) * RET_QK_DIM)
            t = acc[:, hs]
            r = t * cos + pltpu.roll(t, RET_QK_DIM // 2, 1) * sin
            if h >= RET_HEADS:
                r = r * (RET_QK_DIM ** -0.5)
            o_ref[:, hs] = r.astype(BF16)

    @pl.when(j == V_BLK)
    def _():
        o_ref[...] = matmul().astype(BF16)


def _in_proj(x, nw, w_main, w_dt, cos2, sin2, conv_w, conv_b, b_gate, seq):
    n, d = x.shape
    nblk = PROJ_DIM // COL_BLK
    return pl.pallas_call(
        functools.partial(_in_proj_kernel, tiles_per_seq=seq // IN_TM),
        grid=(n // IN_TM, nblk),
        in_specs=[pl.BlockSpec((IN_TM, d), lambda i, j: (i, 0)),
                  pl.BlockSpec((1, d), lambda i, j: (0, 0)),
                  pl.BlockSpec((d, COL_BLK), lambda i, j: (0, j)),
                  pl.BlockSpec((d, LANES), lambda i, j: (0, 0)),
                  pl.BlockSpec((IN_TM, LANES), lambda i, j: (i, 0)),
                  pl.BlockSpec((IN_TM, LANES), lambda i, j: (i, 0)),
                  pl.BlockSpec((SSM_CONV, COL_BLK), lambda i, j: (0, jnp.minimum(j, N_XBC_BLK - 1))),
                  pl.BlockSpec((1, COL_BLK), lambda i, j: (0, jnp.minimum(j, N_XBC_BLK - 1))),
                  pl.BlockSpec((1, COL_BLK), lambda i, j: (0, jnp.clip(j - GATE_BLK0, 0, 1)))],
        out_specs=[pl.BlockSpec((IN_TM, COL_BLK), lambda i, j: (i, j)),
                   pl.BlockSpec((IN_TM, LANES), lambda i, j: (i, 0))],
        out_shape=[jax.ShapeDtypeStruct((n, PROJ_DIM), BF16),
                   jax.ShapeDtypeStruct((n, LANES), F32)],
        scratch_shapes=[pltpu.VMEM((IN_TM, d), BF16),
                        pltpu.VMEM((N_XBC_BLK, SUBLANES, COL_BLK), F32)],
        compiler_params=_params("arbitrary", "arbitrary"),
        name="in_proj",
    )(x, nw, w_main, w_dt, cos2, sin2, conv_w, conv_b, b_gate)


def _split3(x, lane):
    hi = x.astype(BF16).astype(F32)
    r1 = x - hi
    mid = r1.astype(BF16).astype(F32)
    lo = r1 - mid
    return jnp.where(lane < SSM_HEADS, hi, jnp.where(lane < 2 * SSM_HEADS, mid, lo)).astype(BF16)


def _scan_kernel(xbc_ref, zs_ref, qk_ref, v_ref, gs_ref, dt_ref,
                 dtb_ref, alog_ref, dskip_ref, snorm_ref,
                 intra_ref, qdec_ref, kdec_ref, cdec_ref, tri_ref, expand_ref, irep_ref, hmask_ref,
                 yret_ref, yssm_ref, s_scr, h_scr, col_scr, xdt_scr):
    L = CHUNK
    T = SCAN_T
    GD = SSM_GROUP_DIM

    @pl.when(pl.program_id(1) == 0)
    def _():
        s_scr[...] = jnp.zeros_like(s_scr)
        h_scr[...] = jnp.zeros_like(h_scr)

    dtp = dt_ref[...] + dtb_ref[...]
    dt = jnp.maximum(dtp, 0.0) + jnp.log1p(jnp.exp(-jnp.abs(dtp)))
    da = dt * (-LOG2_E * jnp.exp(alog_ref[...]))
    cum = jnp.dot(tri_ref[...], da, preferred_element_type=F32,
                  precision=lax.Precision.HIGHEST)
    lane = lax.broadcasted_iota(jnp.int32, (T, LANES), 1)
    parts = jnp.concatenate([_split3(cum, lane), _split3(dt, lane)], axis=0)
    for gi in range(SSM_GROUPS):
        gs = slice(gi * GD, (gi + 1) * GD)
        col = _dot(parts, expand_ref[:, gs])
        col_scr[:, gs] = col[:T]
        xdt_scr[:, gs] = (xbc_ref[:, gs].astype(F32) * col[T:]).astype(BF16)

    def chunk(c):
        rows = slice(c * L, (c + 1) * L)

        for h in range(RET_HEADS):
            qs = slice(h * RET_QK_DIM, (h + 1) * RET_QK_DIM)
            ks = slice(RET_QK + h * RET_QK_DIM, RET_QK + (h + 1) * RET_QK_DIM)
            vs = slice(h * RET_V_DIM, (h + 1) * RET_V_DIM)
            q = qk_ref[rows, qs]
            k = qk_ref[rows, ks]
            v = v_ref[rows, vs]
            sc = _dot_nt(q, k) * intra_ref[h]
            s_prev = s_scr[h]
            y = _dot(sc.astype(BF16), v) + _dot(q, s_prev.astype(BF16)) * qdec_ref[h]
            kd = (k.astype(F32) * kdec_ref[:, qs]).astype(BF16)
            s_scr[h] = s_prev * cdec_ref[h:h + 1, :] + _dot_tn(kd, v)
            yret_ref[rows, vs] = (_rms(y) * gs_ref[rows, vs].astype(F32)).astype(BF16)

        for gi in range(SSM_GROUPS):
            gs = slice(gi * GD, (gi + 1) * GD)
            bs = slice(SSM_INNER + gi * SSM_STATE, SSM_INNER + (gi + 1) * SSM_STATE)
            cs = slice(SSM_INNER + SSM_BC + gi * SSM_STATE, SSM_INNER + SSM_BC + (gi + 1) * SSM_STATE)
            colc = col_scr[rows, gs]
            rowc = jnp.sum(colc * irep_ref[...], axis=0, keepdims=True)
            seg = jnp.exp2(-jnp.abs(colc - rowc))
            last = colc[L - 1:L, :]
            bm = xbc_ref[rows, bs]
            cm = xbc_ref[rows, cs]
            cb = _dot_nt(cm, jnp.concatenate([bm] * SSM_HPG, axis=0))
            xdt = xdt_scr[rows, gs]
            xblk = jnp.concatenate([xdt * hmask_ref[hj:hj + 1, :] for hj in range(SSM_HPG)],
                                   axis=0)
            h_prev = h_scr[gi]
            y = (_dot((cb * seg).astype(BF16), xblk)
                 + _dot(cm, h_prev.astype(BF16)) * jnp.exp2(colc))
            yg = ((y + xbc_ref[rows, gs].astype(F32) * dskip_ref[:, gs])
                  * zs_ref[rows, gs].astype(F32))
            yssm_ref[rows, gs] = (_rms(yg) * snorm_ref[:, gs]).astype(BF16)
            xw = (xdt.astype(F32) * jnp.exp2(last - colc)).astype(BF16)
            h_scr[gi] = h_prev * jnp.exp2(last) + _dot_tn(bm, xw)

    for c in range(T // L):
        chunk(c)


def _chunk_scan(proj, dt_raw, dt_bias, a_log, d_skip, ssm_norm, tables, batch):
    n = proj.shape[0]
    T = SCAN_T
    nt = n // batch // T
    L = CHUNK

    def row(width, blk):
        return pl.BlockSpec((T, width), lambda b, c: (b * nt + c, blk))

    def full(shape):
        return pl.BlockSpec(shape, lambda b, c: (0,) * len(shape))

    tspecs = [full(t.shape) for t in tables]
    return pl.pallas_call(
        _scan_kernel,
        grid=(batch, nt),
        in_specs=[row(SSM_CONV_DIM, 0), row(SSM_INNER, Z_BLK0 * COL_BLK // SSM_INNER),
                  row(2 * RET_QK, QK_BLK), row(RET_V, V_BLK), row(RET_V, G_BLK), row(LANES, 0),
                  full((1, LANES)), full((1, LANES)), full((1, SSM_INNER)), full((1, SSM_INNER))]
        + tspecs,
        out_specs=[row(RET_V, 0), row(SSM_INNER, 0)],
        out_shape=[jax.ShapeDtypeStruct((n, RET_V), BF16),
                   jax.ShapeDtypeStruct((n, SSM_INNER), BF16)],
        scratch_shapes=[pltpu.VMEM((RET_HEADS, RET_QK_DIM, RET_V_DIM), F32),
                        pltpu.VMEM((SSM_GROUPS, SSM_STATE, SSM_GROUP_DIM), F32),
                        pltpu.VMEM((T, SSM_INNER), F32),
                        pltpu.VMEM((T, SSM_INNER), BF16)],
        compiler_params=_params("arbitrary", "arbitrary"),
        name="chunk_scan",
    )(proj, proj, proj, proj, proj, dt_raw, dt_bias, a_log, d_skip, ssm_norm, *tables)


def _scan_tables():
    L = CHUNK
    idx = jnp.arange(L, dtype=F32)
    log_gamma = jnp.log1p(-(2.0 ** (-5.0 - jnp.arange(RET_HEADS, dtype=F32))))
    rel = jnp.abs(idx[:, None] - idx[None, :])
    intra = jnp.exp(log_gamma[:, None, None] * rel)
    qdec = jnp.exp(log_gamma[:, None] * (idx[None, :] + 1.0))
    qdec = jnp.broadcast_to(qdec[:, :, None], (RET_HEADS, L, RET_V_DIM))
    kdec = jnp.exp(log_gamma[None, :] * (L - 1.0 - idx[:, None]))
    kdec = jnp.repeat(kdec, RET_QK_DIM, axis=1)
    cdec = jnp.broadcast_to(jnp.exp(log_gamma * L)[:, None], (RET_HEADS, RET_V_DIM))
    t = jnp.arange(SCAN_T)
    tri = ((t[:, None] // L == t[None, :] // L) & (t[None, :] <= t[:, None])).astype(F32)
    r = jnp.arange(LANES)
    c = jnp.arange(SSM_INNER)
    expand = ((r[:, None] % SSM_HEADS == c[None, :] // SSM_HEAD_DIM)
              & (r[:, None] < HEAD_COPIES * SSM_HEADS)).astype(BF16)
    gl = jnp.arange(SSM_GROUP_DIM)
    irep = (gl[None, :] % SSM_HEAD_DIM == jnp.arange(L)[:, None]).astype(F32)
    hmask = (gl[None, :] // SSM_HEAD_DIM == jnp.arange(SSM_HPG)[:, None]).astype(BF16)
    return intra, qdec, kdec, cdec, tri, expand, irep, hmask


def _post_kernel(yret_ref, yssm_ref, gate_ref, x_ref, wr_ref, ws_ref, wo_ref,
                 nxa_ref, wq_ref, kv_ref, xo_ref, out_ref):
    merged = (gate_ref[:, :D_MODEL].astype(F32) * _dot(yret_ref[...], wr_ref[...])
              + gate_ref[:, D_MODEL:].astype(F32) * _dot(yssm_ref[...], ws_ref[...]))
    x1 = x_ref[...] + _dot(merged.astype(BF16), wo_ref[...])
    q = _dot((_rms(x1) * nxa_ref[...]).astype(BF16), wq_ref[...]).astype(BF16)
    acc = x1
    for h in range(XA_HEADS):
        hs = slice(h * XA_HEAD_DIM, (h + 1) * XA_HEAD_DIM)
        vs = slice(D_MODEL + h * XA_HEAD_DIM, D_MODEL + (h + 1) * XA_HEAD_DIM)
        s = _dot_nt(q[:, hs], kv_ref[0, 0, :, hs]) * (XA_HEAD_DIM ** -0.5)
        p = jnp.exp(s - jnp.max(s, axis=-1, keepdims=True))
        o = _dot(p.astype(BF16), kv_ref[0, 0, :, vs]) / jnp.sum(p, axis=-1, keepdims=True)
        acc = acc + _dot(o.astype(BF16), xo_ref[hs, :])
    out_ref[...] = acc


def _post(yret, yssm, proj, x, wr, ws, wo, nxa, wq, kv, layer, xo, batch):
    n, d = x.shape
    nt = n // batch // POST_TM
    tm = POST_TM

    def row(width, blk=0):
        return pl.BlockSpec((tm, width), lambda b, t: (b * nt + t, blk))

    def full(shape):
        return pl.BlockSpec(shape, lambda b, t: (0,) * len(shape))

    return pl.pallas_call(
        _post_kernel,
        grid=(batch, nt),
        in_specs=[row(RET_V), row(SSM_INNER), row(2 * d, GATE_BLK0 * COL_BLK // (2 * d)), row(d),
                  full((RET_V, d)), full((SSM_INNER, d)), full((d, d)),
                  full((1, d)), full((d, d)),
                  pl.BlockSpec((1, 1, MEM_LEN, 2 * d), lambda b, t: (layer, b, 0, 0)),
                  full((d, d))],
        out_specs=row(d),
        out_shape=jax.ShapeDtypeStruct((n, d), F32),
        compiler_params=_params("arbitrary", "arbitrary"),
        name="merge_xattn",
    )(yret, yssm, proj, x, wr, ws, wo, nxa, wq, kv, xo)


def _mlp_kernel(x_ref, nw_ref, w1_ref, w2_ref, nf_ref, out_ref, *, final_norm):
    x = x_ref[...]
    h = (_rms(x) * nw_ref[...]).astype(BF16)
    acc = x
    for c in range(D_FF // FF_BLK):
        fs = slice(c * FF_BLK, (c + 1) * FF_BLK)
        a = jnp.maximum(_dot(h, w1_ref[:, fs]), 0.0)
        acc = acc + _dot((a * a).astype(BF16), w2_ref[fs, :])
    if final_norm:
        acc = _rms(acc) * nf_ref[...]
    out_ref[...] = acc


def _mlp(x, nw, w1, w2, nf, final_norm):
    n, d = x.shape
    tm = MLP_TM
    return pl.pallas_call(
        functools.partial(_mlp_kernel, final_norm=final_norm),
        grid=(n // tm,),
        in_specs=[pl.BlockSpec((tm, d), lambda i: (i, 0)),
                  pl.BlockSpec((1, d), lambda i: (0, 0)),
                  pl.BlockSpec((d, D_FF), lambda i: (0, 0)),
                  pl.BlockSpec((D_FF, d), lambda i: (0, 0)),
                  pl.BlockSpec((1, d), lambda i: (0, 0))],
        out_specs=pl.BlockSpec((tm, d), lambda i: (i, 0)),
        out_shape=jax.ShapeDtypeStruct((n, d), F32),
        compiler_params=_params("arbitrary"),
        name="sq_relu_mlp",
    )(x, nw, w1, w2, nf)


def _replicate_heads(v):
    return jnp.pad(jnp.tile(v, HEAD_COPIES), (0, LANES - HEAD_COPIES * SSM_HEADS)).reshape(1, LANES)


def kernel(x, mem, positions, norm_mix, w_in, b_gate, conv_w, conv_b, dt_bias, a_log, d_skip,
           ssm_norm, w_br_ret, w_br_ssm, w_out, norm_xa, norm_mem, xa_wq, xa_wkv, xa_wo,
           norm_mlp, mlp_w1, mlp_w2, norm_final):
    b, s, d = x.shape
    depth = w_in.shape[0]
    n = b * s
    assert d == D_MODEL and s % IN_TM == 0 and mem.shape[1] == MEM_LEN

    o_z = 2 * RET_QK + 2 * RET_V
    o_xbc = o_z + SSM_INNER
    o_dt = o_xbc + SSM_CONV_DIM
    o_gate = o_dt + SSM_HEADS

    cos2, sin2 = _rope_tables(positions)
    kv = _memory_kv(mem, norm_mem, xa_wkv.astype(BF16))
    tables = _scan_tables()

    xf = x.reshape(n, d)
    for l in range(depth):
        wl = w_in[l]
        w_main = jnp.concatenate(
            [wl[:, o_xbc:o_dt], wl[:, o_z:o_xbc], wl[:, o_gate:], wl[:, :o_z]], axis=1).astype(BF16)
        w_dt = jnp.pad(jnp.tile(wl[:, o_dt:o_gate], (1, HEAD_COPIES)),
                       ((0, 0), (0, LANES - HEAD_COPIES * SSM_HEADS))).astype(BF16)
        proj, dt_raw = _in_proj(xf, norm_mix[l].reshape(1, d), w_main, w_dt, cos2, sin2,
                                conv_w[l], conv_b[l].reshape(1, -1), b_gate[l].reshape(1, -1), s)
        yret, yssm = _chunk_scan(
            proj, dt_raw, _replicate_heads(dt_bias[l]), _replicate_heads(a_log[l]),
            jnp.repeat(d_skip[l], SSM_HEAD_DIM).reshape(1, SSM_INNER),
            ssm_norm[l].reshape(1, SSM_INNER), tables, b)
        xf = _post(yret, yssm, proj, xf, w_br_ret[l].astype(BF16), w_br_ssm[l].astype(BF16),
                   w_out[l].astype(BF16), norm_xa[l].reshape(1, d), xa_wq[l].astype(BF16), kv, l,
                   xa_wo[l].astype(BF16), b)
        xf = _mlp(xf, norm_mlp[l].reshape(1, d), mlp_w1[l].astype(BF16), mlp_w2[l].astype(BF16),
                  norm_final.reshape(1, d), final_norm=(l == depth - 1))
    return xf.reshape(b, s, d)
```

```python
import functools

import jax
import jax.numpy as jnp
from jax import lax
from jax.experimental import pallas as pl
from jax.experimental.pallas import tpu as pltpu

F32 = jnp.float32
BF16 = jnp.bfloat16

D_MODEL = 1024
CHUNK = 64
MEM_LEN = 256
EPS = 1e-6
RET_HEADS = 4
RET_QK_DIM = 128
RET_V_DIM = 256
RET_QK = RET_HEADS * RET_QK_DIM
RET_V = RET_HEADS * RET_V_DIM
ROPE_THETA = 10000.0
SSM_INNER = 2 * D_MODEL
SSM_HEAD_DIM = 64
SSM_HEADS = SSM_INNER // SSM_HEAD_DIM
SSM_GROUPS = 8
SSM_HPG = SSM_HEADS // SSM_GROUPS
SSM_GROUP_DIM = SSM_HPG * SSM_HEAD_DIM
SSM_STATE = 128
SSM_CONV = 4
assert SSM_CONV == 4
LOG2_E = 1.4426950408889634
SSM_BC = SSM_GROUPS * SSM_STATE
SSM_CONV_DIM = SSM_INNER + 2 * SSM_BC
XA_HEADS = 4
XA_HEAD_DIM = D_MODEL // XA_HEADS
D_FF = 4 * D_MODEL

LANES = 128
SUBLANES = 8
VMEM_LIMIT = 56 * 1024 * 1024

COL_BLK = 1024
N_COL_BLK = 11
PROJ_W = (N_COL_BLK + 1) * COL_BLK
G_OFF = 1 * COL_BLK
Z_OFF = 2 * COL_BLK
XBC_OFF = 4 * COL_BLK
GATE_OFF = 8 * COL_BLK
QK_OFF = 10 * COL_BLK
V_OFF = 11 * COL_BLK
assert (RET_V, SSM_INNER, SSM_CONV_DIM, 2 * D_MODEL, 2 * RET_QK, RET_V) == (
    COL_BLK, 2 * COL_BLK, 4 * COL_BLK, 2 * COL_BLK, COL_BLK, COL_BLK)
HEAD_COPIES = 3

IN_TM = 1024
SCAN_T = 256
POST_TM = 512
MLP_TM = 512
FF_BLK = 1024


def _params(*sem):
    return pltpu.CompilerParams(dimension_semantics=sem, vmem_limit_bytes=VMEM_LIMIT)


def _rms(x):
    return x * lax.rsqrt(jnp.mean(x * x, axis=-1, keepdims=True) + EPS)


def _sigmoid(x):
    return 0.5 * jnp.tanh(0.5 * x) + 0.5


def _silu_of_half(h):
    return h + h * jnp.tanh(h)


def _dot(a, b):
    return jnp.dot(a, b, preferred_element_type=F32)


def _dot_nt(a, b):
    return lax.dot_general(a, b, (((1,), (1,)), ((), ())), preferred_element_type=F32)


def _dot_tn(a, b):
    return lax.dot_general(a, b, (((0,), (0,)), ((), ())), preferred_element_type=F32)


def _rope_kernel(pos_ref, freq_ref, cos_ref, sin_ref):
    ang = pos_ref[...] * freq_ref[...]
    cos_ref[...] = jnp.cos(ang)
    sin_ref[...] = jnp.sin(ang)


def _rope_tables(positions):
    b, s = positions.shape
    n = b * s
    half = RET_QK_DIM // 2
    inv_freq = ROPE_THETA ** (-jnp.arange(0, RET_QK_DIM, 2, dtype=F32) / RET_QK_DIM)
    pos = jnp.broadcast_to(positions.astype(F32).reshape(n // 2, 2, 1), (n // 2, 2, half))
    pos = pos.reshape(n // 2, LANES)
    freq = jnp.concatenate([inv_freq, inv_freq]).reshape(1, LANES)
    tm = min(2048, n // 2)
    cos, sin = pl.pallas_call(
        _rope_kernel,
        grid=(n // 2 // tm,),
        in_specs=[pl.BlockSpec((tm, LANES), lambda i: (i, 0)),
                  pl.BlockSpec((1, LANES), lambda i: (0, 0))],
        out_specs=[pl.BlockSpec((tm, LANES), lambda i: (i, 0)),
                   pl.BlockSpec((tm, LANES), lambda i: (i, 0))],
        out_shape=[jax.ShapeDtypeStruct((n // 2, LANES), F32)] * 2,
        compiler_params=_params("parallel"),
        name="rope_tables",
    )(pos, freq)
    cos = cos.reshape(n, half)
    sin = sin.reshape(n, half)
    return jnp.concatenate([cos, cos], axis=-1), jnp.concatenate([-sin, sin], axis=-1)


def _kv_kernel(mem_ref, nw_ref, w_ref, kv_ref):
    h = (_rms(mem_ref[0]) * nw_ref[0]).astype(BF16)
    kv_ref[0, 0] = _dot(h, w_ref[0]).astype(BF16)


def _memory_kv(mem, norm_mem, wkv):
    depth = wkv.shape[0]
    b, m, d = mem.shape
    return pl.pallas_call(
        _kv_kernel,
        grid=(depth, b),
        in_specs=[pl.BlockSpec((1, m, d), lambda l, i: (i, 0, 0)),
                  pl.BlockSpec((1, 1, d), lambda l, i: (l, 0, 0)),
                  pl.BlockSpec((1, d, 2 * d), lambda l, i: (l, 0, 0))],
        out_specs=pl.BlockSpec((1, 1, m, 2 * d), lambda l, i: (l, i, 0, 0)),
        out_shape=jax.ShapeDtypeStruct((depth, b, m, 2 * d), BF16),
        compiler_params=_params("arbitrary", "arbitrary"),
        name="memory_kv",
    )(mem, norm_mem.reshape(depth, 1, d), wkv)


def _in_proj_kernel(x_ref, nw_ref, w_ref, wdt_ref, cos_ref, sin_ref, convw_ref, convb_ref,
                    bg_ref, o_ref, dt_ref, h_scr, raw0, raw1, tail_scr, *, tiles_per_seq):
    i = pl.program_id(0)
    s = pl.program_id(1)
    lo = slice(0, COL_BLK)
    hi = slice(COL_BLK, 2 * COL_BLK)

    def matmul(cols, raw):
        raw[...] = _dot(h_scr[...], w_ref[:, cols])

    def silu(raw, cols):
        o_ref[:, cols] = _silu_of_half(0.5 * raw[...]).astype(BF16)

    def gate(raw, cols):
        o_ref[:, cols] = _sigmoid(raw[...] + bg_ref[:, cols]).astype(BF16)

    def conv(raw, cols, t):
        ext = jnp.concatenate([tail_scr[t], raw[...]], axis=0)
        ext1 = pltpu.roll(ext, 1, 0)
        w = 0.5 * convw_ref[:, cols]
        older = w[1:2] * ext + w[0:1] * ext1
        half = (w[3:4] * ext + w[2:3] * ext1
                + pltpu.roll(older, 2, 0))[SUBLANES:, :] + 0.5 * convb_ref[:, cols]
        o_ref[:, cols] = _silu_of_half(half).astype(BF16)
        tail_scr[t] = raw[IN_TM - SUBLANES:, :]

    def rope(raw, cols):
        cos = cos_ref[...]
        sin = sin_ref[...]
        for h in range(2 * RET_HEADS):
            hs = slice(h * RET_QK_DIM, (h + 1) * RET_QK_DIM)
            t = raw[:, hs]
            r = t * cos + pltpu.roll(t, RET_QK_DIM // 2, 1) * sin
            if h >= RET_HEADS:
                r = r * (RET_QK_DIM ** -0.5)
            o_ref[:, cols.start + hs.start:cols.start + hs.stop] = r.astype(BF16)

    @pl.when(jnp.logical_and(s == 0, i % tiles_per_seq == 0))
    def _():
        tail_scr[...] = jnp.zeros_like(tail_scr)

    @pl.when(s == 0)
    def _():
        h = (_rms(x_ref[...]) * nw_ref[...]).astype(BF16)
        h_scr[...] = h
        dt_ref[...] = _dot(h, wdt_ref[...])
        o_ref[:, lo] = jnp.zeros((IN_TM, COL_BLK), BF16)
        matmul(lo, raw0)
        matmul(hi, raw1)
        silu(raw0, hi)

    @pl.when(s == 1)
    def _():
        matmul(lo, raw0)
        silu(raw1, lo)
        matmul(hi, raw1)
        silu(raw0, hi)

    @pl.when(jnp.logical_or(s == 2, s == 3))
    def _():
        t0 = 2 * (s - 2)
        matmul(lo, raw0)
        conv(raw1, lo, t0)
        matmul(hi, raw1)
        conv(raw0, hi, t0 + 1)

    @pl.when(s == 4)
    def _():
        matmul(lo, raw0)
        gate(raw1, lo)
        matmul(hi, raw1)
        gate(raw0, hi)

    @pl.when(s == 5)
    def _():
        matmul(lo, raw0)
        rope(raw1, lo)
        o_ref[:, hi] = raw0[...].astype(BF16)


def _in_proj(x, nw, w_main, w_dt, cos2, sin2, conv_w, conv_b, b_gate, seq):
    n, d = x.shape
    pair = 2 * COL_BLK
    conv_step0 = XBC_OFF // pair
    return pl.pallas_call(
        functools.partial(_in_proj_kernel, tiles_per_seq=seq // IN_TM),
        grid=(n // IN_TM, PROJ_W // pair),
        in_specs=[pl.BlockSpec((IN_TM, d), lambda i, s: (i, 0)),
                  pl.BlockSpec((1, d), lambda i, s: (0, 0)),
                  pl.BlockSpec((d, pair), lambda i, s: (0, s)),
                  pl.BlockSpec((d, LANES), lambda i, s: (0, 0)),
                  pl.BlockSpec((IN_TM, LANES), lambda i, s: (i, 0)),
                  pl.BlockSpec((IN_TM, LANES), lambda i, s: (i, 0)),
                  pl.BlockSpec((SSM_CONV, pair), lambda i, s: (0, jnp.clip(s - conv_step0, 0, 1))),
                  pl.BlockSpec((1, pair), lambda i, s: (0, jnp.clip(s - conv_step0, 0, 1))),
                  pl.BlockSpec((1, pair), lambda i, s: (0, 0))],
        out_specs=[pl.BlockSpec((IN_TM, pair), lambda i, s: (i, s)),
                   pl.BlockSpec((IN_TM, LANES), lambda i, s: (i, 0))],
        out_shape=[jax.ShapeDtypeStruct((n, PROJ_W), BF16),
                   jax.ShapeDtypeStruct((n, LANES), F32)],
        scratch_shapes=[pltpu.VMEM((IN_TM, d), BF16),
                        pltpu.VMEM((IN_TM, COL_BLK), F32),
                        pltpu.VMEM((IN_TM, COL_BLK), F32),
                        pltpu.VMEM((SSM_CONV_DIM // COL_BLK, SUBLANES, COL_BLK), F32)],
        compiler_params=_params("arbitrary", "arbitrary"),
        name="in_proj",
    )(x, nw, w_main, w_dt, cos2, sin2, conv_w, conv_b, b_gate)


def _split3(x, lane):
    hi = x.astype(BF16).astype(F32)
    r1 = x - hi
    mid = r1.astype(BF16).astype(F32)
    lo = r1 - mid
    return jnp.where(lane < SSM_HEADS, hi, jnp.where(lane < 2 * SSM_HEADS, mid, lo)).astype(BF16)


def _scan_kernel(xbc_ref, zs_ref, qk_ref, v_ref, gs_ref, dt_ref,
                 dtb_ref, alog_ref, dskip_ref, snorm_ref,
                 intra_ref, qdec_ref, kdec_ref, cdec_ref, tri_ref, expand_ref, irep_ref, hmask_ref,
                 yret_ref, yssm_ref, s_scr, h_scr, col_scr, xdt_scr):
    L = CHUNK
    T = SCAN_T
    GD = SSM_GROUP_DIM

    @pl.when(pl.program_id(1) == 0)
    def _():
        s_scr[...] = jnp.zeros_like(s_scr)
        h_scr[...] = jnp.zeros_like(h_scr)

    dtp = dt_ref[...] + dtb_ref[...]
    dt = jnp.maximum(dtp, 0.0) + jnp.log1p(jnp.exp(-jnp.abs(dtp)))
    da = dt * (-LOG2_E * jnp.exp(alog_ref[...]))
    cum = jnp.dot(tri_ref[...], da, preferred_element_type=F32,
                  precision=lax.Precision.HIGHEST)
    lane = lax.broadcasted_iota(jnp.int32, (T, LANES), 1)
    parts = jnp.concatenate([_split3(cum, lane), _split3(dt, lane)], axis=0)
    for gi in range(SSM_GROUPS):
        gs = slice(gi * GD, (gi + 1) * GD)
        col = _dot(parts, expand_ref[:, gs])
        col_scr[:, gs] = col[:T]
        xdt_scr[:, gs] = (xbc_ref[:, gs].astype(F32) * col[T:]).astype(BF16)

    def chunk(c):
        rows = slice(c * L, (c + 1) * L)

        for h in range(RET_HEADS):
            qs = slice(h * RET_QK_DIM, (h + 1) * RET_QK_DIM)
            ks = slice(RET_QK + h * RET_QK_DIM, RET_QK + (h + 1) * RET_QK_DIM)
            vs = slice(h * RET_V_DIM, (h + 1) * RET_V_DIM)
            q = qk_ref[rows, qs]
            k = qk_ref[rows, ks]
            v = v_ref[rows, vs]
            sc = _dot_nt(q, k) * intra_ref[h]
            s_prev = s_scr[h]
            y = _dot(sc.astype(BF16), v) + _dot(q, s_prev.astype(BF16)) * qdec_ref[h]
            kd = (k.astype(F32) * kdec_ref[:, qs]).astype(BF16)
            s_scr[h] = s_prev * cdec_ref[h:h + 1, :] + _dot_tn(kd, v)
            yret_ref[rows, vs] = (_rms(y) * gs_ref[rows, vs].astype(F32)).astype(BF16)

        for gi in range(SSM_GROUPS):
            gs = slice(gi * GD, (gi + 1) * GD)
            bs = slice(SSM_INNER + gi * SSM_STATE, SSM_INNER + (gi + 1) * SSM_STATE)
            cs = slice(SSM_INNER + SSM_BC + gi * SSM_STATE, SSM_INNER + SSM_BC + (gi + 1) * SSM_STATE)
            colc = col_scr[rows, gs]
            rowc = jnp.sum(colc * irep_ref[...], axis=0, keepdims=True)
            seg = jnp.exp2(-jnp.abs(colc - rowc))
            last = colc[L - 1:L, :]
            bm = xbc_ref[rows, bs]
            cm = xbc_ref[rows, cs]
            cb = _dot_nt(cm, jnp.concatenate([bm] * SSM_HPG, axis=0))
            xdt = xdt_scr[rows, gs]
            xblk = jnp.concatenate([xdt * hmask_ref[hj:hj + 1, :] for hj in range(SSM_HPG)],
                                   axis=0)
            h_prev = h_scr[gi]
            y = (_dot((cb * seg).astype(BF16), xblk)
                 + _dot(cm, h_prev.astype(BF16)) * jnp.exp2(colc))
            yg = ((y + xbc_ref[rows, gs].astype(F32) * dskip_ref[:, gs])
                  * zs_ref[rows, gs].astype(F32))
            yssm_ref[rows, gs] = (_rms(yg) * snorm_ref[:, gs]).astype(BF16)
            xw = (xdt.astype(F32) * jnp.exp2(last - colc)).astype(BF16)
            h_scr[gi] = h_prev * jnp.exp2(last) + _dot_tn(bm, xw)

    for c in range(T // L):
        chunk(c)


def _chunk_scan(proj, dt_raw, dt_bias, a_log, d_skip, ssm_norm, tables, batch):
    n = proj.shape[0]
    T = SCAN_T
    nt = n // batch // T
    L = CHUNK

    def row(width, blk):
        return pl.BlockSpec((T, width), lambda b, c: (b * nt + c, blk))

    def full(shape):
        return pl.BlockSpec(shape, lambda b, c: (0,) * len(shape))

    tspecs = [full(t.shape) for t in tables]
    return pl.pallas_call(
        _scan_kernel,
        grid=(batch, nt),
        in_specs=[row(SSM_CONV_DIM, XBC_OFF // SSM_CONV_DIM), row(SSM_INNER, Z_OFF // SSM_INNER),
                  row(2 * RET_QK, QK_OFF // (2 * RET_QK)), row(RET_V, V_OFF // RET_V),
                  row(RET_V, G_OFF // RET_V), row(LANES, 0),
                  full((1, LANES)), full((1, LANES)), full((1, SSM_INNER)), full((1, SSM_INNER))]
        + tspecs,
        out_specs=[row(RET_V, 0), row(SSM_INNER, 0)],
        out_shape=[jax.ShapeDtypeStruct((n, RET_V), BF16),
                   jax.ShapeDtypeStruct((n, SSM_INNER), BF16)],
        scratch_shapes=[pltpu.VMEM((RET_HEADS, RET_QK_DIM, RET_V_DIM), F32),
                        pltpu.VMEM((SSM_GROUPS, SSM_STATE, SSM_GROUP_DIM), F32),
                        pltpu.VMEM((T, SSM_INNER), F32),
                        pltpu.VMEM((T, SSM_INNER), BF16)],
        compiler_params=_params("arbitrary", "arbitrary"),
        name="chunk_scan",
    )(proj, proj, proj, proj, proj, dt_raw, dt_bias, a_log, d_skip, ssm_norm, *tables)


def _scan_tables():
    L = CHUNK
    idx = jnp.arange(L, dtype=F32)
    log_gamma = jnp.log1p(-(2.0 ** (-5.0 - jnp.arange(RET_HEADS, dtype=F32))))
    rel = jnp.abs(idx[:, None] - idx[None, :])
    intra = jnp.exp(log_gamma[:, None, None] * rel)
    qdec = jnp.exp(log_gamma[:, None] * (idx[None, :] + 1.0))
    qdec = jnp.broadcast_to(qdec[:, :, None], (RET_HEADS, L, RET_V_DIM))
    kdec = jnp.exp(log_gamma[None, :] * (L - 1.0 - idx[:, None]))
    kdec = jnp.repeat(kdec, RET_QK_DIM, axis=1)
    cdec = jnp.broadcast_to(jnp.exp(log_gamma * L)[:, None], (RET_HEADS, RET_V_DIM))
    t = jnp.arange(SCAN_T)
    tri = ((t[:, None] // L == t[None, :] // L) & (t[None, :] <= t[:, None])).astype(F32)
    r = jnp.arange(LANES)
    c = jnp.arange(SSM_INNER)
    expand = ((r[:, None] % SSM_HEADS == c[None, :] // SSM_HEAD_DIM)
              & (r[:, None] < HEAD_COPIES * SSM_HEADS)).astype(BF16)
    gl = jnp.arange(SSM_GROUP_DIM)
    irep = (gl[None, :] % SSM_HEAD_DIM == jnp.arange(L)[:, None]).astype(F32)
    hmask = (gl[None, :] // SSM_HEAD_DIM == jnp.arange(SSM_HPG)[:, None]).astype(BF16)
    return intra, qdec, kdec, cdec, tri, expand, irep, hmask


def _post_kernel(yret_ref, yssm_ref, gate_ref, x_ref, wr_ref, ws_ref, wo_ref,
                 nxa_ref, wq_ref, kv_ref, xo_ref, out_ref):
    merged = (gate_ref[:, :D_MODEL].astype(F32) * _dot(yret_ref[...], wr_ref[...])
              + gate_ref[:, D_MODEL:].astype(F32) * _dot(yssm_ref[...], ws_ref[...]))
    x1 = x_ref[...] + _dot(merged.astype(BF16), wo_ref[...])
    q = _dot((_rms(x1) * nxa_ref[...]).astype(BF16), wq_ref[...]).astype(BF16)
    acc = x1
    for h in range(XA_HEADS):
        hs = slice(h * XA_HEAD_DIM, (h + 1) * XA_HEAD_DIM)
        vs = slice(D_MODEL + h * XA_HEAD_DIM, D_MODEL + (h + 1) * XA_HEAD_DIM)
        s = _dot_nt(q[:, hs], kv_ref[0, 0, :, hs]) * (XA_HEAD_DIM ** -0.5)
        p = jnp.exp(s - jnp.max(s, axis=-1, keepdims=True))
        o = _dot(p.astype(BF16), kv_ref[0, 0, :, vs]) / jnp.sum(p, axis=-1, keepdims=True)
        acc = acc + _dot(o.astype(BF16), xo_ref[hs, :])
    out_ref[...] = acc


def _post(yret, yssm, proj, x, wr, ws, wo, nxa, wq, kv, layer, xo, batch):
    n, d = x.shape
    nt = n // batch // POST_TM
    tm = POST_TM

    def row(width, blk=0):
        return pl.BlockSpec((tm, width), lambda b, t: (b * nt + t, blk))

    def full(shape):
        return pl.BlockSpec(shape, lambda b, t: (0,) * len(shape))

    return pl.pallas_call(
        _post_kernel,
        grid=(batch, nt),
        in_specs=[row(RET_V), row(SSM_INNER), row(2 * d, GATE_OFF // (2 * d)), row(d),
                  full((RET_V, d)), full((SSM_INNER, d)), full((d, d)),
                  full((1, d)), full((d, d)),
                  pl.BlockSpec((1, 1, MEM_LEN, 2 * d), lambda b, t: (layer, b, 0, 0)),
                  full((d, d))],
        out_specs=row(d),
        out_shape=jax.ShapeDtypeStruct((n, d), F32),
        compiler_params=_params("arbitrary", "arbitrary"),
        name="merge_xattn",
    )(yret, yssm, proj, x, wr, ws, wo, nxa, wq, kv, xo)


def _mlp_kernel(x_ref, nw_ref, w1_ref, w2_ref, nf_ref, out_ref, *, final_norm):
    x = x_ref[...]
    h = (_rms(x) * nw_ref[...]).astype(BF16)
    acc = x
    for c in range(D_FF // FF_BLK):
        fs = slice(c * FF_BLK, (c + 1) * FF_BLK)
        a = jnp.maximum(_dot(h, w1_ref[:, fs]), 0.0)
        acc = acc + _dot((a * a).astype(BF16), w2_ref[fs, :])
    if final_norm:
        acc = _rms(acc) * nf_ref[...]
    out_ref[...] = acc


def _mlp(x, nw, w1, w2, nf, final_norm):
    n, d = x.shape
    tm = MLP_TM
    return pl.pallas_call(
        functools.partial(_mlp_kernel, final_norm=final_norm),
        grid=(n // tm,),
        in_specs=[pl.BlockSpec((tm, d), lambda i: (i, 0)),
                  pl.BlockSpec((1, d), lambda i: (0, 0)),
                  pl.BlockSpec((d, D_FF), lambda i: (0, 0)),
                  pl.BlockSpec((D_FF, d), lambda i: (0, 0)),
                  pl.BlockSpec((1, d), lambda i: (0, 0))],
        out_specs=pl.BlockSpec((tm, d), lambda i: (i, 0)),
        out_shape=jax.ShapeDtypeStruct((n, d), F32),
        compiler_params=_params("arbitrary"),
        name="sq_relu_mlp",
    )(x, nw, w1, w2, nf)


def _replicate_heads(v):
    return jnp.pad(jnp.tile(v, HEAD_COPIES), (0, LANES - HEAD_COPIES * SSM_HEADS)).reshape(1, LANES)


def kernel(x, mem, positions, norm_mix, w_in, b_gate, conv_w, conv_b, dt_bias, a_log, d_skip,
           ssm_norm, w_br_ret, w_br_ssm, w_out, norm_xa, norm_mem, xa_wq, xa_wkv, xa_wo,
           norm_mlp, mlp_w1, mlp_w2, norm_final):
    b, s, d = x.shape
    depth = w_in.shape[0]
    n = b * s
    assert d == D_MODEL and s % IN_TM == 0 and mem.shape[1] == MEM_LEN

    o_v = 2 * RET_QK
    o_g = o_v + RET_V
    o_z = o_g + RET_V
    o_xbc = o_z + SSM_INNER
    o_dt = o_xbc + SSM_CONV_DIM
    o_gate = o_dt + SSM_HEADS

    cos2, sin2 = _rope_tables(positions)
    kv = _memory_kv(mem, norm_mem, xa_wkv.astype(BF16))
    tables = _scan_tables()

    xf = x.reshape(n, d)
    for l in range(depth):
        wl = w_in[l]
        w_main = jnp.concatenate(
            [wl[:, o_g:o_z], wl[:, o_z:o_xbc], wl[:, o_xbc:o_dt], wl[:, o_gate:], wl[:, :o_v],
             wl[:, o_v:o_g], jnp.zeros((d, COL_BLK), wl.dtype)], axis=1).astype(BF16)
        w_dt = jnp.pad(jnp.tile(wl[:, o_dt:o_gate], (1, HEAD_COPIES)),
                       ((0, 0), (0, LANES - HEAD_COPIES * SSM_HEADS))).astype(BF16)
        proj, dt_raw = _in_proj(xf, norm_mix[l].reshape(1, d), w_main, w_dt, cos2, sin2,
                                conv_w[l], conv_b[l].reshape(1, -1), b_gate[l].reshape(1, -1), s)
        yret, yssm = _chunk_scan(
            proj, dt_raw, _replicate_heads(dt_bias[l]), _replicate_heads(a_log[l]),
            jnp.repeat(d_skip[l], SSM_HEAD_DIM).reshape(1, SSM_INNER),
            ssm_norm[l].reshape(1, SSM_INNER), tables, b)
        xf = _post(yret, yssm, proj, xf, w_br_ret[l].astype(BF16), w_br_ssm[l].astype(BF16),
                   w_out[l].astype(BF16), norm_xa[l].reshape(1, d), xa_wq[l].astype(BF16), kv, l,
                   xa_wo[l].astype(BF16), b)
        xf = _mlp(xf, norm_mlp[l].reshape(1, d), mlp_w1[l].astype(BF16), mlp_w2[l].astype(BF16),
                  norm_final.reshape(1, d), final_norm=(l == depth - 1))
    return xf.reshape(b, s, d)
```

```python
import functools

import jax
import jax.numpy as jnp
from jax import lax
from jax.experimental import pallas as pl
from jax.experimental.pallas import tpu as pltpu

F32 = jnp.float32
BF16 = jnp.bfloat16

D_MODEL = 1024
CHUNK = 64
MEM_LEN = 256
EPS = 1e-6
RET_HEADS = 4
RET_QK_DIM = 128
RET_V_DIM = 256
RET_QK = RET_HEADS * RET_QK_DIM
RET_V = RET_HEADS * RET_V_DIM
ROPE_THETA = 10000.0
SSM_INNER = 2 * D_MODEL
SSM_HEAD_DIM = 64
SSM_HEADS = SSM_INNER // SSM_HEAD_DIM
SSM_GROUPS = 8
SSM_HPG = SSM_HEADS // SSM_GROUPS
SSM_GROUP_DIM = SSM_HPG * SSM_HEAD_DIM
SSM_STATE = 128
SSM_CONV = 4
assert SSM_CONV == 4
LOG2_E = 1.4426950408889634
SSM_BC = SSM_GROUPS * SSM_STATE
SSM_CONV_DIM = SSM_INNER + 2 * SSM_BC
XA_HEADS = 4
XA_HEAD_DIM = D_MODEL // XA_HEADS
D_FF = 4 * D_MODEL

LANES = 128
SUBLANES = 8
VMEM_LIMIT = 56 * 1024 * 1024

COL_BLK = 1024
N_COL_BLK = 11
PROJ_W = (N_COL_BLK + 1) * COL_BLK
G_OFF = 1 * COL_BLK
Z_OFF = 2 * COL_BLK
XBC_OFF = 4 * COL_BLK
GATE_OFF = 8 * COL_BLK
QK_OFF = 10 * COL_BLK
V_OFF = 11 * COL_BLK
assert (RET_V, SSM_INNER, SSM_CONV_DIM, 2 * D_MODEL, 2 * RET_QK, RET_V) == (
    COL_BLK, 2 * COL_BLK, 4 * COL_BLK, 2 * COL_BLK, COL_BLK, COL_BLK)
HEAD_COPIES = 3

IN_TM = 1024
SCAN_T = 256
POST_TM = 512
MLP_TM = 512
FF_BLK = 1024


def _params(*sem):
    return pltpu.CompilerParams(dimension_semantics=sem, vmem_limit_bytes=VMEM_LIMIT)


def _rms(x):
    return x * lax.rsqrt(jnp.mean(x * x, axis=-1, keepdims=True) + EPS)


def _sigmoid(x):
    return 0.5 * jnp.tanh(0.5 * x) + 0.5


def _silu_of_half(h):
    return h + h * jnp.tanh(h)


def _dot(a, b):
    return jnp.dot(a, b, preferred_element_type=F32)


def _dot_nt(a, b):
    return lax.dot_general(a, b, (((1,), (1,)), ((), ())), preferred_element_type=F32)


def _dot_tn(a, b):
    return lax.dot_general(a, b, (((0,), (0,)), ((), ())), preferred_element_type=F32)


def _rope_kernel(pos_ref, freq_ref, cos_ref, sin_ref):
    ang = pos_ref[...] * freq_ref[...]
    cos_ref[...] = jnp.cos(ang)
    sin_ref[...] = jnp.sin(ang)


def _rope_tables(positions):
    b, s = positions.shape
    n = b * s
    half = RET_QK_DIM // 2
    inv_freq = ROPE_THETA ** (-jnp.arange(0, RET_QK_DIM, 2, dtype=F32) / RET_QK_DIM)
    pos = jnp.broadcast_to(positions.astype(F32).reshape(n // 2, 2, 1), (n // 2, 2, half))
    pos = pos.reshape(n // 2, LANES)
    freq = jnp.concatenate([inv_freq, inv_freq]).reshape(1, LANES)
    tm = min(2048, n // 2)
    cos, sin = pl.pallas_call(
        _rope_kernel,
        grid=(n // 2 // tm,),
        in_specs=[pl.BlockSpec((tm, LANES), lambda i: (i, 0)),
                  pl.BlockSpec((1, LANES), lambda i: (0, 0))],
        out_specs=[pl.BlockSpec((tm, LANES), lambda i: (i, 0)),
                   pl.BlockSpec((tm, LANES), lambda i: (i, 0))],
        out_shape=[jax.ShapeDtypeStruct((n // 2, LANES), F32)] * 2,
        compiler_params=_params("parallel"),
        name="rope_tables",
    )(pos, freq)
    cos = cos.reshape(n, half)
    sin = sin.reshape(n, half)
    return jnp.concatenate([cos, cos], axis=-1), jnp.concatenate([-sin, sin], axis=-1)


def _kv_kernel(mem_ref, nw_ref, w_ref, kv_ref):
    h = (_rms(mem_ref[0]) * nw_ref[0]).astype(BF16)
    kv_ref[0, 0] = _dot(h, w_ref[0]).astype(BF16)


def _memory_kv(mem, norm_mem, wkv):
    depth = wkv.shape[0]
    b, m, d = mem.shape
    return pl.pallas_call(
        _kv_kernel,
        grid=(depth, b),
        in_specs=[pl.BlockSpec((1, m, d), lambda l, i: (i, 0, 0)),
                  pl.BlockSpec((1, 1, d), lambda l, i: (l, 0, 0)),
                  pl.BlockSpec((1, d, 2 * d), lambda l, i: (l, 0, 0))],
        out_specs=pl.BlockSpec((1, 1, m, 2 * d), lambda l, i: (l, i, 0, 0)),
        out_shape=jax.ShapeDtypeStruct((depth, b, m, 2 * d), BF16),
        compiler_params=_params("arbitrary", "arbitrary"),
        name="memory_kv",
    )(mem, norm_mem.reshape(depth, 1, d), wkv)


def _in_proj_kernel(x_ref, nw_ref, w_ref, wdt_ref, cos_ref, sin_ref, convw_ref, convb_ref,
                    bg_ref, o_ref, dt_ref, h_scr, raw0, raw1, tail_scr, *, tiles_per_seq):
    i = pl.program_id(0)
    s = pl.program_id(1)
    lo = slice(0, COL_BLK)
    hi = slice(COL_BLK, 2 * COL_BLK)

    def matmul(cols, raw):
        raw[...] = _dot(h_scr[...], w_ref[:, cols])

    def silu(raw, cols):
        o_ref[:, cols] = _silu_of_half(0.5 * raw[...]).astype(BF16)

    def gate(raw, cols):
        o_ref[:, cols] = _sigmoid(raw[...] + bg_ref[:, cols]).astype(BF16)

    def conv(raw, cols, t):
        ext = jnp.concatenate([tail_scr[t], raw[...]], axis=0)
        ext1 = pltpu.roll(ext, 1, 0)
        w = 0.5 * convw_ref[:, cols]
        older = w[1:2] * ext + w[0:1] * ext1
        half = (w[3:4] * ext + w[2:3] * ext1
                + pltpu.roll(older, 2, 0))[SUBLANES:, :] + 0.5 * convb_ref[:, cols]
        o_ref[:, cols] = _silu_of_half(half).astype(BF16)
        tail_scr[t] = raw[IN_TM - SUBLANES:, :]

    def rope(raw, cols):
        cos = cos_ref[...]
        sin = sin_ref[...]
        for h in range(2 * RET_HEADS):
            hs = slice(h * RET_QK_DIM, (h + 1) * RET_QK_DIM)
            t = raw[:, hs]
            r = t * cos + pltpu.roll(t, RET_QK_DIM // 2, 1) * sin
            if h >= RET_HEADS:
                r = r * (RET_QK_DIM ** -0.5)
            o_ref[:, cols.start + hs.start:cols.start + hs.stop] = r.astype(BF16)

    @pl.when(jnp.logical_and(s == 0, i % tiles_per_seq == 0))
    def _():
        tail_scr[...] = jnp.zeros_like(tail_scr)

    @pl.when(s == 0)
    def _():
        h = (_rms(x_ref[...]) * nw_ref[...]).astype(BF16)
        h_scr[...] = h
        dt_ref[...] = _dot(h, wdt_ref[...])
        o_ref[:, lo] = jnp.zeros((IN_TM, COL_BLK), BF16)
        matmul(lo, raw0)
        matmul(hi, raw1)
        silu(raw0, hi)

    @pl.when(s == 1)
    def _():
        matmul(lo, raw0)
        silu(raw1, lo)
        matmul(hi, raw1)
        silu(raw0, hi)

    @pl.when(jnp.logical_or(s == 2, s == 3))
    def _():
        t0 = 2 * (s - 2)
        matmul(lo, raw0)
        conv(raw1, lo, t0)
        matmul(hi, raw1)
        conv(raw0, hi, t0 + 1)

    @pl.when(s == 4)
    def _():
        matmul(lo, raw0)
        gate(raw1, lo)
        matmul(hi, raw1)
        gate(raw0, hi)

    @pl.when(s == 5)
    def _():
        matmul(lo, raw0)
        rope(raw1, lo)
        o_ref[:, hi] = raw0[...].astype(BF16)


def _in_proj(x, nw, w_main, w_dt, cos2, sin2, conv_w, conv_b, b_gate, seq):
    n, d = x.shape
    pair = 2 * COL_BLK
    conv_step0 = XBC_OFF // pair
    return pl.pallas_call(
        functools.partial(_in_proj_kernel, tiles_per_seq=seq // IN_TM),
        grid=(n // IN_TM, PROJ_W // pair),
        in_specs=[pl.BlockSpec((IN_TM, d), lambda i, s: (i, 0)),
                  pl.BlockSpec((1, d), lambda i, s: (0, 0)),
                  pl.BlockSpec((d, pair), lambda i, s: (0, s)),
                  pl.BlockSpec((d, LANES), lambda i, s: (0, 0)),
                  pl.BlockSpec((IN_TM, LANES), lambda i, s: (i, 0)),
                  pl.BlockSpec((IN_TM, LANES), lambda i, s: (i, 0)),
                  pl.BlockSpec((SSM_CONV, pair), lambda i, s: (0, jnp.clip(s - conv_step0, 0, 1))),
                  pl.BlockSpec((1, pair), lambda i, s: (0, jnp.clip(s - conv_step0, 0, 1))),
                  pl.BlockSpec((1, pair), lambda i, s: (0, 0))],
        out_specs=[pl.BlockSpec((IN_TM, pair), lambda i, s: (i, s)),
                   pl.BlockSpec((IN_TM, LANES), lambda i, s: (i, 0))],
        out_shape=[jax.ShapeDtypeStruct((n, PROJ_W), BF16),
                   jax.ShapeDtypeStruct((n, LANES), F32)],
        scratch_shapes=[pltpu.VMEM((IN_TM, d), BF16),
                        pltpu.VMEM((IN_TM, COL_BLK), F32),
                        pltpu.VMEM((IN_TM, COL_BLK), F32),
                        pltpu.VMEM((SSM_CONV_DIM // COL_BLK, SUBLANES, COL_BLK), F32)],
        compiler_params=_params("arbitrary", "arbitrary"),
        name="in_proj",
    )(x, nw, w_main, w_dt, cos2, sin2, conv_w, conv_b, b_gate)


def _split3(x, lane):
    hi = x.astype(BF16).astype(F32)
    r1 = x - hi
    mid = r1.astype(BF16).astype(F32)
    lo = r1 - mid
    return jnp.where(lane < SSM_HEADS, hi, jnp.where(lane < 2 * SSM_HEADS, mid, lo)).astype(BF16)


def _scan_kernel(xbc_ref, zs_ref, qk_ref, v_ref, gs_ref, dt_ref,
                 dtb_ref, alog_ref, dskip_ref, snorm_ref,
                 intra_ref, qdec_ref, kdec_ref, cdec_ref, tri_ref, expand_ref, irep_ref, hmask_ref,
                 yret_ref, yssm_ref, s_scr, h_scr, col_scr, xdt_scr):
    L = CHUNK
    T = SCAN_T
    GD = SSM_GROUP_DIM

    @pl.when(pl.program_id(1) == 0)
    def _():
        s_scr[...] = jnp.zeros_like(s_scr)
        h_scr[...] = jnp.zeros_like(h_scr)

    dtp = dt_ref[...] + dtb_ref[...]
    dt = jnp.maximum(dtp, 0.0) + jnp.log1p(jnp.exp(-jnp.abs(dtp)))
    da = dt * (-LOG2_E * jnp.exp(alog_ref[...]))
    da_hi = da.astype(BF16)
    da_r1 = da - da_hi.astype(F32)
    da_mid = da_r1.astype(BF16)
    da_lo = (da_r1 - da_mid.astype(F32)).astype(BF16)
    tri = tri_ref[...]
    cum = _dot(tri, da_hi) + _dot(tri, da_mid) + _dot(tri, da_lo)
    lane = lax.broadcasted_iota(jnp.int32, (T, LANES), 1)
    parts = jnp.concatenate([_split3(cum, lane), _split3(dt, lane)], axis=0)
    for gi in range(SSM_GROUPS):
        gs = slice(gi * GD, (gi + 1) * GD)
        col = _dot(parts, expand_ref[:, gs])
        col_scr[:, gs] = col[:T]
        xdt_scr[:, gs] = (xbc_ref[:, gs].astype(F32) * col[T:]).astype(BF16)

    def chunk(c):
        rows = slice(c * L, (c + 1) * L)

        for h in range(RET_HEADS):
            qs = slice(h * RET_QK_DIM, (h + 1) * RET_QK_DIM)
            ks = slice(RET_QK + h * RET_QK_DIM, RET_QK + (h + 1) * RET_QK_DIM)
            vs = slice(h * RET_V_DIM, (h + 1) * RET_V_DIM)
            q = qk_ref[rows, qs]
            k = qk_ref[rows, ks]
            v = v_ref[rows, vs]
            sc = _dot_nt(q, k) * intra_ref[h]
            s_prev = s_scr[h]
            y = _dot(sc.astype(BF16), v) + _dot(q, s_prev.astype(BF16)) * qdec_ref[h]
            kd = (k.astype(F32) * kdec_ref[:, qs]).astype(BF16)
            s_scr[h] = s_prev * cdec_ref[h:h + 1, :] + _dot_tn(kd, v)
            yret_ref[rows, vs] = (_rms(y) * gs_ref[rows, vs].astype(F32)).astype(BF16)

        for gi in range(SSM_GROUPS):
            gs = slice(gi * GD, (gi + 1) * GD)
            bs = slice(SSM_INNER + gi * SSM_STATE, SSM_INNER + (gi + 1) * SSM_STATE)
            cs = slice(SSM_INNER + SSM_BC + gi * SSM_STATE, SSM_INNER + SSM_BC + (gi + 1) * SSM_STATE)
            colc = col_scr[rows, gs]
            rowc = jnp.sum(colc * irep_ref[...], axis=0, keepdims=True)
            seg = jnp.exp2(-jnp.abs(colc - rowc))
            last = colc[L - 1:L, :]
            bm = xbc_ref[rows, bs]
            cm = xbc_ref[rows, cs]
            cb = _dot_nt(cm, jnp.concatenate([bm] * SSM_HPG, axis=0))
            xdt = xdt_scr[rows, gs]
            xblk = jnp.concatenate([xdt * hmask_ref[hj:hj + 1, :] for hj in range(SSM_HPG)],
                                   axis=0)
            h_prev = h_scr[gi]
            y = (_dot((cb * seg).astype(BF16), xblk)
                 + _dot(cm, h_prev.astype(BF16)) * jnp.exp2(colc))
            yg = ((y + xbc_ref[rows, gs].astype(F32) * dskip_ref[:, gs])
                  * zs_ref[rows, gs].astype(F32))
            yssm_ref[rows, gs] = (_rms(yg) * snorm_ref[:, gs]).astype(BF16)
            xw = xdt * jnp.exp2(last - colc).astype(BF16)
            h_scr[gi] = h_prev * jnp.exp2(last) + _dot_tn(bm, xw)

    for c in range(T // L):
        chunk(c)


def _chunk_scan(proj, dt_raw, dt_bias, a_log, d_skip, ssm_norm, tables, batch):
    n = proj.shape[0]
    T = SCAN_T
    nt = n // batch // T
    L = CHUNK

    def row(width, blk):
        return pl.BlockSpec((T, width), lambda b, c: (b * nt + c, blk))

    def full(shape):
        return pl.BlockSpec(shape, lambda b, c: (0,) * len(shape))

    tspecs = [full(t.shape) for t in tables]
    return pl.pallas_call(
        _scan_kernel,
        grid=(batch, nt),
        in_specs=[row(SSM_CONV_DIM, XBC_OFF // SSM_CONV_DIM), row(SSM_INNER, Z_OFF // SSM_INNER),
                  row(2 * RET_QK, QK_OFF // (2 * RET_QK)), row(RET_V, V_OFF // RET_V),
                  row(RET_V, G_OFF // RET_V), row(LANES, 0),
                  full((1, LANES)), full((1, LANES)), full((1, SSM_INNER)), full((1, SSM_INNER))]
        + tspecs,
        out_specs=[row(RET_V, 0), row(SSM_INNER, 0)],
        out_shape=[jax.ShapeDtypeStruct((n, RET_V), BF16),
                   jax.ShapeDtypeStruct((n, SSM_INNER), BF16)],
        scratch_shapes=[pltpu.VMEM((RET_HEADS, RET_QK_DIM, RET_V_DIM), F32),
                        pltpu.VMEM((SSM_GROUPS, SSM_STATE, SSM_GROUP_DIM), F32),
                        pltpu.VMEM((T, SSM_INNER), F32),
                        pltpu.VMEM((T, SSM_INNER), BF16)],
        compiler_params=_params("arbitrary", "arbitrary"),
        name="chunk_scan",
    )(proj, proj, proj, proj, proj, dt_raw, dt_bias, a_log, d_skip, ssm_norm, *tables)


def _scan_tables():
    L = CHUNK
    idx = jnp.arange(L, dtype=F32)
    log_gamma = jnp.log1p(-(2.0 ** (-5.0 - jnp.arange(RET_HEADS, dtype=F32))))
    rel = jnp.abs(idx[:, None] - idx[None, :])
    intra = jnp.exp(log_gamma[:, None, None] * rel)
    qdec = jnp.exp(log_gamma[:, None] * (idx[None, :] + 1.0))
    qdec = jnp.broadcast_to(qdec[:, :, None], (RET_HEADS, L, RET_V_DIM))
    kdec = jnp.exp(log_gamma[None, :] * (L - 1.0 - idx[:, None]))
    kdec = jnp.repeat(kdec, RET_QK_DIM, axis=1)
    cdec = jnp.broadcast_to(jnp.exp(log_gamma * L)[:, None], (RET_HEADS, RET_V_DIM))
    t = jnp.arange(SCAN_T)
    tri = ((t[:, None] // L == t[None, :] // L) & (t[None, :] <= t[:, None])).astype(BF16)
    r = jnp.arange(LANES)
    c = jnp.arange(SSM_INNER)
    expand = ((r[:, None] % SSM_HEADS == c[None, :] // SSM_HEAD_DIM)
              & (r[:, None] < HEAD_COPIES * SSM_HEADS)).astype(BF16)
    gl = jnp.arange(SSM_GROUP_DIM)
    irep = (gl[None, :] % SSM_HEAD_DIM == jnp.arange(L)[:, None]).astype(F32)
    hmask = (gl[None, :] // SSM_HEAD_DIM == jnp.arange(SSM_HPG)[:, None]).astype(BF16)
    return intra, qdec, kdec, cdec, tri, expand, irep, hmask


def _post_kernel(yret_ref, yssm_ref, gate_ref, x_ref, wr_ref, ws_ref, wo_ref,
                 nxa_ref, wq_ref, kv_ref, xo_ref, out_ref):
    merged = (gate_ref[:, :D_MODEL].astype(F32) * _dot(yret_ref[...], wr_ref[...])
              + gate_ref[:, D_MODEL:].astype(F32) * _dot(yssm_ref[...], ws_ref[...]))
    x1 = x_ref[...] + _dot(merged.astype(BF16), wo_ref[...])
    q = _dot((_rms(x1) * nxa_ref[...]).astype(BF16), wq_ref[...]).astype(BF16)
    acc = x1
    for h in range(XA_HEADS):
        hs = slice(h * XA_HEAD_DIM, (h + 1) * XA_HEAD_DIM)
        vs = slice(D_MODEL + h * XA_HEAD_DIM, D_MODEL + (h + 1) * XA_HEAD_DIM)
        s = _dot_nt(q[:, hs], kv_ref[0, 0, :, hs]) * (XA_HEAD_DIM ** -0.5)
        p = jnp.exp(s - jnp.max(s, axis=-1, keepdims=True))
        o = _dot(p.astype(BF16), kv_ref[0, 0, :, vs]) / jnp.sum(p, axis=-1, keepdims=True)
        acc = acc + _dot(o.astype(BF16), xo_ref[hs, :])
    out_ref[...] = acc


def _post(yret, yssm, proj, x, wr, ws, wo, nxa, wq, kv, layer, xo, batch):
    n, d = x.shape
    nt = n // batch // POST_TM
    tm = POST_TM

    def row(width, blk=0):
        return pl.BlockSpec((tm, width), lambda b, t: (b * nt + t, blk))

    def full(shape):
        return pl.BlockSpec(shape, lambda b, t: (0,) * len(shape))

    return pl.pallas_call(
        _post_kernel,
        grid=(batch, nt),
        in_specs=[row(RET_V), row(SSM_INNER), row(2 * d, GATE_OFF // (2 * d)), row(d),
                  full((RET_V, d)), full((SSM_INNER, d)), full((d, d)),
                  full((1, d)), full((d, d)),
                  pl.BlockSpec((1, 1, MEM_LEN, 2 * d), lambda b, t: (layer, b, 0, 0)),
                  full((d, d))],
        out_specs=row(d),
        out_shape=jax.ShapeDtypeStruct((n, d), F32),
        compiler_params=_params("arbitrary", "arbitrary"),
        name="merge_xattn",
    )(yret, yssm, proj, x, wr, ws, wo, nxa, wq, kv, xo)


def _mlp_kernel(x_ref, nw_ref, w1_ref, w2_ref, nf_ref, out_ref, *, final_norm):
    x = x_ref[...]
    h = (_rms(x) * nw_ref[...]).astype(BF16)
    acc = x
    for c in range(D_FF // FF_BLK):
        fs = slice(c * FF_BLK, (c + 1) * FF_BLK)
        a = jnp.maximum(_dot(h, w1_ref[:, fs]), 0.0)
        acc = acc + _dot((a * a).astype(BF16), w2_ref[fs, :])
    if final_norm:
        acc = _rms(acc) * nf_ref[...]
    out_ref[...] = acc


def _mlp(x, nw, w1, w2, nf, final_norm):
    n, d = x.shape
    tm = MLP_TM
    return pl.pallas_call(
        functools.partial(_mlp_kernel, final_norm=final_norm),
        grid=(n // tm,),
        in_specs=[pl.BlockSpec((tm, d), lambda i: (i, 0)),
                  pl.BlockSpec((1, d), lambda i: (0, 0)),
                  pl.BlockSpec((d, D_FF), lambda i: (0, 0)),
                  pl.BlockSpec((D_FF, d), lambda i: (0, 0)),
                  pl.BlockSpec((1, d), lambda i: (0, 0))],
        out_specs=pl.BlockSpec((tm, d), lambda i: (i, 0)),
        out_shape=jax.ShapeDtypeStruct((n, d), F32),
        compiler_params=_params("arbitrary"),
        name="sq_relu_mlp",
    )(x, nw, w1, w2, nf)


def _replicate_heads(v):
    return jnp.pad(jnp.tile(v, HEAD_COPIES), (0, LANES - HEAD_COPIES * SSM_HEADS)).reshape(1, LANES)


def kernel(x, mem, positions, norm_mix, w_in, b_gate, conv_w, conv_b, dt_bias, a_log, d_skip,
           ssm_norm, w_br_ret, w_br_ssm, w_out, norm_xa, norm_mem, xa_wq, xa_wkv, xa_wo,
           norm_mlp, mlp_w1, mlp_w2, norm_final):
    b, s, d = x.shape
    depth = w_in.shape[0]
    n = b * s
    assert d == D_MODEL and s % IN_TM == 0 and mem.shape[1] == MEM_LEN

    o_v = 2 * RET_QK
    o_g = o_v + RET_V
    o_z = o_g + RET_V
    o_xbc = o_z + SSM_INNER
    o_dt = o_xbc + SSM_CONV_DIM
    o_gate = o_dt + SSM_HEADS

    cos2, sin2 = _rope_tables(positions)
    kv = _memory_kv(mem, norm_mem, xa_wkv.astype(BF16))
    tables = _scan_tables()

    xf = x.reshape(n, d)
    for l in range(depth):
        wl = w_in[l]
        w_main = jnp.concatenate(
            [wl[:, o_g:o_z], wl[:, o_z:o_xbc], wl[:, o_xbc:o_dt], wl[:, o_gate:], wl[:, :o_v],
             wl[:, o_v:o_g], jnp.zeros((d, COL_BLK), wl.dtype)], axis=1).astype(BF16)
        w_dt = jnp.pad(jnp.tile(wl[:, o_dt:o_gate], (1, HEAD_COPIES)),
                       ((0, 0), (0, LANES - HEAD_COPIES * SSM_HEADS))).astype(BF16)
        proj, dt_raw = _in_proj(xf, norm_mix[l].reshape(1, d), w_main, w_dt, cos2, sin2,
                                conv_w[l], conv_b[l].reshape(1, -1), b_gate[l].reshape(1, -1), s)
        yret, yssm = _chunk_scan(
            proj, dt_raw, _replicate_heads(dt_bias[l]), _replicate_heads(a_log[l]),
            jnp.repeat(d_skip[l], SSM_HEAD_DIM).reshape(1, SSM_INNER),
            ssm_norm[l].reshape(1, SSM_INNER), tables, b)
        xf = _post(yret, yssm, proj, xf, w_br_ret[l].astype(BF16), w_br_ssm[l].astype(BF16),
                   w_out[l].astype(BF16), norm_xa[l].reshape(1, d), xa_wq[l].astype(BF16), kv, l,
                   xa_wo[l].astype(BF16), b)
        xf = _mlp(xf, norm_mlp[l].reshape(1, d), mlp_w1[l].astype(BF16), mlp_w2[l].astype(BF16),
                  norm_final.reshape(1, d), final_norm=(l == depth - 1))
    return xf.reshape(b, s, d)
```
